```python
import math
import jax, jax.numpy as jnp
from jax import lax
import numpy as np

D_MODEL = 1024
BATCH = 8
SEQ = 4096
DEPTH = 4

A_HEADS = 4
A_QK_DIM = 64
A_V_DIM = 2 * A_QK_DIM
A_WIDTH = A_HEADS * A_V_DIM
B_PAIRS = ((128, 1), (512, 4), (2048, 16))
B_GROUPS = len(B_PAIRS)
B_HEADS = 4
B_HEAD_DIM = 128
B_WIDTH = B_HEADS * B_HEAD_DIM
B_BLOCK = 128
Q_BLOCK = 128
A_Q_COLS = A_HEADS * 2 * A_QK_DIM
A_K_COLS = A_HEADS * 2 * A_QK_DIM
A_V_COLS = A_HEADS * A_V_DIM
B_COLS = B_GROUPS * B_WIDTH
GATE_COLS = 2 * D_MODEL
IN_COLS = A_Q_COLS + A_K_COLS + A_V_COLS + 3 * B_COLS + GATE_COLS
D_FF = 2816
CONV_WIDTH = 3
ROPE_THETA = 10000.0
NORM_EPS = 1e-6

kernel_name = "hybrid_diffattn_dilated_convglu"


def rms_norm(x, g):
    xf = x.astype(jnp.float32)
    y = xf * lax.rsqrt(jnp.mean(xf * xf, axis=-1, keepdims=True) + NORM_EPS)
    return (y * g.astype(jnp.float32)).astype(x.dtype)


def rope_tables(positions, dim):
    inv = ROPE_THETA ** (-jnp.arange(0, dim, 2, dtype=jnp.float32) / dim)
    ang = positions.astype(jnp.float32)[..., None] * inv
    return jnp.cos(ang)[:, :, None, None, :], jnp.sin(ang)[:, :, None, None, :]


def apply_rope(t, cos, sin):
    tf = t.astype(jnp.float32)
    t1, t2 = jnp.split(tf, 2, axis=-1)
    return jnp.concatenate([t1 * cos - t2 * sin, t2 * cos + t1 * sin], axis=-1).astype(t.dtype)


def diff_attention(q, k, v, lam):
    b, s, h, _, dqk = q.shape
    nq = s // Q_BLOCK
    scale = dqk ** -0.5
    qb = q.reshape(b, nq, Q_BLOCK, h, 2, dqk).transpose(1, 0, 2, 3, 4, 5)
    kpos = jnp.arange(s)

    def one_block(args):
        qi, i = args
        sc = jnp.einsum('bqhcd,bkhcd->bchqk', qi, k).astype(jnp.float32) * scale
        qpos = i * Q_BLOCK + jnp.arange(Q_BLOCK)
        causal = kpos[None, :] <= qpos[:, None]
        p = jax.nn.softmax(jnp.where(causal, sc, -jnp.inf), axis=-1)
        a = p[:, 0] - lam * p[:, 1]
        return jnp.einsum('bhqk,bkhd->bqhd', a.astype(v.dtype), v)

    o = lax.map(one_block, (qb, jnp.arange(nq)))
    return o.transpose(1, 0, 2, 3, 4).reshape(b, s, h, v.shape[-1])


def dilated_group(q, k, v, window, dil):
    b, s, h, d = q.shape
    L = s // dil
    nwin = window // dil
    nb = -(-L // B_BLOCK)
    Lp = nb * B_BLOCK

    def to_blocks(t):
        t = t.reshape(b, L, dil, h, d).transpose(0, 2, 1, 3, 4)
        t = jnp.pad(t, ((0, 0), (0, 0), (0, Lp - L), (0, 0), (0, 0)))
        return t.reshape(b, dil, nb, B_BLOCK, h, d)

    def with_prev(t):
        prev = jnp.pad(t[:, :, :-1], ((0, 0), (0, 0), (1, 0), (0, 0), (0, 0), (0, 0)))
        return jnp.concatenate([prev, t], axis=3)

    qb, kb, vb = to_blocks(q), to_blocks(k), to_blocks(v)
    kk, vv = with_prev(kb), with_prev(vb)
    sc = jnp.einsum('brnqhd,brnkhd->brnhqk', qb, kk).astype(jnp.float32) * (d ** -0.5)
    qi = jnp.arange(B_BLOCK)[:, None]
    kj = jnp.arange(2 * B_BLOCK)[None, :]
    rel = qi - kj + B_BLOCK
    kabs = jnp.arange(nb)[:, None, None] * B_BLOCK + kj - B_BLOCK
    valid = (rel >= 0) & (rel <= nwin) & (kabs >= 0)
    sc = jnp.where(valid[None, None, :, None], sc, -jnp.inf)
    m = jnp.max(sc, axis=-1, keepdims=True)
    p = jnp.exp(sc - m)
    den = jnp.sum(p, axis=-1, keepdims=True)
    o = jnp.einsum('brnhqk,brnkhd->brnqhd', (p / den).astype(v.dtype), vv)
    lse = (m + jnp.log(den))[..., 0].transpose(0, 1, 2, 4, 3)

    def from_blocks(t):
        t = t.reshape((b, dil, Lp) + t.shape[4:])[:, :, :L]
        t = jnp.moveaxis(t, 1, 2)
        return t.reshape((b, s) + t.shape[3:])

    return from_blocks(o), from_blocks(lse)


def causal_depthwise_conv(u, w, bias):
    s = u.shape[1]
    up = jnp.pad(u, ((0, 0), (CONV_WIDTH - 1, 0), (0, 0)))
    out = sum(up[:, j:j + s] * w[j] for j in range(CONV_WIDTH))
    return out + bias


def setup_inputs(seed: int = 0) -> dict:
    key = jax.random.key(seed)
    ks = jax.random.split(key, 20)
    f32 = jnp.float32

    def nrm(k, shape, scale):
        return jax.random.normal(k, shape, f32) * scale

    def gain(k, shape):
        return 1.0 + 0.05 * jax.random.normal(k, shape, f32)

    x = jax.random.normal(ks[0], (BATCH, SEQ, D_MODEL), f32)
    offset = jax.random.randint(ks[1], (BATCH, 1), 0, 1024, dtype=jnp.int32)
    positions = (offset + jnp.arange(SEQ, dtype=jnp.int32)[None, :]).astype(jnp.int32)
    return {
        "x": x,
        "positions": positions,
        "pre_mix_g": gain(ks[2], (DEPTH, D_MODEL)),
        "w_in": nrm(ks[3], (DEPTH, D_MODEL, IN_COLS), D_MODEL ** -0.5),
        "diff_lambda": nrm(ks[4], (DEPTH, 4, A_QK_DIM), 0.1),
        "diff_head_g": gain(ks[5], (DEPTH, A_V_DIM)),
        "w_a_out": nrm(ks[6], (DEPTH, A_WIDTH, D_MODEL), A_WIDTH ** -0.5),
        "w_b_out": nrm(ks[7], (DEPTH, B_WIDTH, D_MODEL), B_WIDTH ** -0.5),
        "w_mix_out": nrm(ks[8], (DEPTH, D_MODEL, D_MODEL), D_MODEL ** -0.5),
        "post_mix_g": gain(ks[9], (DEPTH, D_MODEL)),
        "pre_ffn_g": gain(ks[10], (DEPTH, D_MODEL)),
        "w_up": nrm(ks[11], (DEPTH, D_MODEL, 2 * D_FF), D_MODEL ** -0.5),
        "conv_w": nrm(ks[12], (DEPTH, CONV_WIDTH, 2 * D_FF), CONV_WIDTH ** -0.5),
        "conv_b": nrm(ks[13], (DEPTH, 2 * D_FF), 0.01),
        "w_down": nrm(ks[14], (DEPTH, D_FF, D_MODEL), D_FF ** -0.5),
        "post_ffn_g": gain(ks[15], (DEPTH, D_MODEL)),
    }


def reference(x, positions, pre_mix_g, w_in, diff_lambda, diff_head_g, w_a_out, w_b_out,
              w_mix_out, post_mix_g, pre_ffn_g, w_up, conv_w, conv_b, w_down, post_ffn_g):
    b, s, _ = x.shape
    cos_a, sin_a = rope_tables(positions, A_QK_DIM)
    cos_b, sin_b = rope_tables(positions, B_HEAD_DIM)
    split_at = np.cumsum([A_Q_COLS, A_K_COLS, A_V_COLS, B_COLS, B_COLS, B_COLS]).tolist()

    for l in range(DEPTH):
        h = rms_norm(x, pre_mix_g[l])
        proj = h @ w_in[l]
        qa, ka, va, qb, kb, vb, gates = jnp.split(proj, split_at, axis=-1)

        qa = apply_rope(qa.reshape(b, s, A_HEADS, 2, A_QK_DIM), cos_a, sin_a)
        ka = apply_rope(ka.reshape(b, s, A_HEADS, 2, A_QK_DIM), cos_a, sin_a)
        va = va.reshape(b, s, A_HEADS, A_V_DIM)
        lam_init = 0.8 - 0.6 * math.exp(-0.3 * l)
        lv = diff_lambda[l].astype(jnp.float32)
        lam = jnp.exp(jnp.sum(lv[0] * lv[1])) - jnp.exp(jnp.sum(lv[2] * lv[3])) + lam_init
        oa = diff_attention(qa, ka, va, lam)
        oa = rms_norm(oa, diff_head_g[l]) * (1.0 - lam_init)
        ya = oa.reshape(b, s, A_WIDTH) @ w_a_out[l]

        qb = apply_rope(qb.reshape(b, s, B_GROUPS, B_HEADS, B_HEAD_DIM), cos_b, sin_b)
        kb = apply_rope(kb.reshape(b, s, B_GROUPS, B_HEADS, B_HEAD_DIM), cos_b, sin_b)
        vb = vb.reshape(b, s, B_GROUPS, B_HEADS, B_HEAD_DIM)
        outs, lses = [], []
        for g, (window, dil) in enumerate(B_PAIRS):
            o_g, lse_g = dilated_group(qb[:, :, g], kb[:, :, g], vb[:, :, g], window, dil)
            outs.append(o_g)
            lses.append(lse_g)
        wts = jax.nn.softmax(jnp.stack(lses, axis=0), axis=0)
        ob = jnp.einsum('gbsh,gbshd->bshd', wts.astype(x.dtype), jnp.stack(outs, axis=0))
        yb = ob.reshape(b, s, B_WIDTH) @ w_b_out[l]

        g_a, g_b = jnp.split(jax.nn.sigmoid(gates), 2, axis=-1)
        mix = (g_a * ya + g_b * yb) @ w_mix_out[l]
        x = x + rms_norm(mix, post_mix_g[l])

        h = rms_norm(x, pre_ffn_g[l])
        u = causal_depthwise_conv(h @ w_up[l], conv_w[l], conv_b[l])
        gate, val = jnp.split(u, 2, axis=-1)
        y = (jax.nn.gelu(gate, approximate=True) * val) @ w_down[l]
        x = x + rms_norm(y, post_ffn_g[l])
    return x
```

```python
import functools
import math

import jax
import jax.numpy as jnp
import numpy as np
from jax import lax
from jax.experimental import pallas as pl
from jax.experimental.pallas import tpu as pltpu

F32 = jnp.float32
BF16 = jnp.bfloat16

D_MODEL = 1024
N_HEADS = 4
HEAD_W = 128
A_QK_DIM = 64
B_PAIRS = ((128, 1), (512, 4), (2048, 16))
B_BLOCK = 128
D_FF = 2816
ROPE_THETA = 10000.0
NORM_EPS = 1e-6
COL_TILE = N_HEADS * HEAD_W
N_TAB = 5
VMEM_LIMIT = 56 * 1024 * 1024

K_AQ, K_AK, K_V, K_BQ, K_BK, K_GATE = range(6)


def _cparams(sem):
    return pltpu.CompilerParams(dimension_semantics=sem, vmem_limit_bytes=VMEM_LIMIT)


def _rms(x, g):
    return x * lax.rsqrt(jnp.mean(x * x, axis=-1, keepdims=True) + NORM_EPS) * g


def _rope_table_kernel(pos_ref, f_ref, o_ref):
    pos = pos_ref[...].astype(F32)
    ang_a = pos * f_ref[0:1, :]
    ang_b = pos * f_ref[1:2, :]
    sin_a = jnp.sin(ang_a)
    o_ref[0] = jnp.cos(ang_a)
    o_ref[1] = sin_a * f_ref[2:3, :]
    o_ref[2] = sin_a * f_ref[3:4, :]
    o_ref[3] = jnp.cos(ang_b)
    o_ref[4] = jnp.sin(ang_b) * f_ref[4:5, :]


def _rope_tables(positions):
    t = positions.size
    tm = 1024
    lane = np.arange(HEAD_W)
    inv_a = ROPE_THETA ** (-jnp.arange(0, A_QK_DIM, 2, dtype=F32) / A_QK_DIM)
    inv_b = ROPE_THETA ** (-jnp.arange(0, HEAD_W, 2, dtype=F32) / HEAD_W)
    half_a = A_QK_DIM // 2
    lo = (lane % A_QK_DIM) < half_a
    rows = [
        inv_a[lane % half_a],
        inv_b[lane % (HEAD_W // 2)],
        jnp.asarray(np.where(lo, -1.0, 0.0), F32),
        jnp.asarray(np.where(lo, 0.0, 1.0), F32),
        jnp.asarray(np.where(lane < HEAD_W // 2, -1.0, 1.0), F32),
        jnp.zeros((HEAD_W,), F32), jnp.zeros((HEAD_W,), F32), jnp.zeros((HEAD_W,), F32),
    ]
    ftab = jnp.stack(rows, axis=0)
    return pl.pallas_call(
        _rope_table_kernel,
        grid=(t // tm,),
        in_specs=[pl.BlockSpec((tm, 1), lambda i: (i, 0)),
                  pl.BlockSpec((8, HEAD_W), lambda i: (0, 0))],
        out_specs=pl.BlockSpec((N_TAB, tm, HEAD_W), lambda i: (0, i, 0)),
        out_shape=jax.ShapeDtypeStruct((N_TAB, t, HEAD_W), F32),
        compiler_params=_cparams(("parallel",)),
        name="rope_tables",
    )(positions.reshape(t, 1), ftab)


def _proj_kernel(x_ref, g_ref, w_ref, tab_ref, o_ref, h_scr, acc_scr, *, dil, kinds, tm):
    j = pl.program_id(2)
    rows = tm // dil

    @pl.when(j == 0)
    def _():
        h_scr[...] = _rms(x_ref[...], g_ref[...]).astype(BF16)

    acc = jnp.dot(h_scr[...], w_ref[...], preferred_element_type=F32)
    if dil > 1:
        for hh in range(N_HEADS):
            acc_scr[hh] = acc[:, hh * HEAD_W:(hh + 1) * HEAD_W]

    def epilogue(kind):
        for hh in range(N_HEADS):
            for r in range(dil):
                sel = slice(None) if dil == 1 else pl.ds(r, rows, stride=dil)
                a = acc[:, hh * HEAD_W:(hh + 1) * HEAD_W] if dil == 1 else acc_scr[hh, sel, :]
                tab = lambda k: tab_ref[k, sel, :]
                if kind in (K_AQ, K_AK):
                    out = (a * tab(0) + pltpu.roll(a, HEAD_W - A_QK_DIM // 2, 1) * tab(1)
                           + pltpu.roll(a, A_QK_DIM // 2, 1) * tab(2))
                    if kind == K_AQ:
                        out = out * (A_QK_DIM ** -0.5)
                elif kind in (K_BQ, K_BK):
                    out = a * tab(3) + pltpu.roll(a, HEAD_W // 2, 1) * tab(4)
                elif kind == K_GATE:
                    out = jax.nn.sigmoid(a)
                else:
                    out = a
                o_ref[hh, r] = out.astype(BF16)

    for kind in sorted(set(kinds)):
        cond = functools.reduce(jnp.logical_or, [j == c for c, k in enumerate(kinds) if k == kind])
        pl.when(cond)(functools.partial(epilogue, kind))


def _project(xf, g, w, tabs, *, batch, seq, dil, kinds, tm):
    ncol = len(kinds)
    nrow = seq // tm
    rows = tm // dil
    kern = functools.partial(_proj_kernel, dil=dil, kinds=tuple(kinds), tm=tm)
    return pl.pallas_call(
        kern,
        grid=(batch, nrow, ncol),
        in_specs=[
            pl.BlockSpec((tm, D_MODEL), lambda b, i, j: (b * nrow + i, 0)),
            pl.BlockSpec((1, D_MODEL), lambda b, i, j: (0, 0)),
            pl.BlockSpec((D_MODEL, COL_TILE), lambda b, i, j: (0, j)),
            pl.BlockSpec((N_TAB, tm, HEAD_W), lambda b, i, j: (0, b * nrow + i, 0)),
        ],
        out_specs=pl.BlockSpec((N_HEADS, None, dil, rows, HEAD_W), lambda b, i, j: (j, b, 0, i, 0)),
        out_shape=jax.ShapeDtypeStruct((ncol * N_HEADS, batch, dil, seq // dil, HEAD_W), BF16),
        scratch_shapes=[pltpu.VMEM((tm, D_MODEL), BF16),
                        pltpu.VMEM((N_HEADS, tm if dil > 1 else 8, HEAD_W), F32)],
        compiler_params=_cparams(("parallel", "parallel", "arbitrary")),
        name=f"in_proj_dil{dil}",
    )(xf, g, w, tabs)


def _diff_attn_kernel(lam_ref, g_ref, q_ref, k_ref, v_ref, o_ref,
                      m1, l1, a1, m2, l2, a2, *, blk, lam_init):
    i = pl.program_id(2)
    q = q_ref[...]
    lane = lax.broadcasted_iota(jnp.int32, q.shape, 1)
    zero = jnp.zeros_like(q)
    comps = ((jnp.where(lane < A_QK_DIM, q, zero), m1, l1, a1),
             (jnp.where(lane >= A_QK_DIM, q, zero), m2, l2, a2))
    for _, m_ref, l_ref, a_ref in comps:
        m_ref[...] = jnp.full(m_ref.shape, -jnp.inf, F32)
        l_ref[...] = jnp.zeros(l_ref.shape, F32)
        a_ref[...] = jnp.zeros(a_ref.shape, F32)

    def step(j, masked):
        k = k_ref[pl.ds(pl.multiple_of(j * blk, blk), blk), :]
        v = v_ref[pl.ds(pl.multiple_of(j * blk, blk), blk), :]
        for qc, m_ref, l_ref, a_ref in comps:
            s = lax.dot_general(qc, k, (((1,), (1,)), ((), ())), preferred_element_type=F32)
            if masked:
                row = lax.broadcasted_iota(jnp.int32, s.shape, 0)
                col = lax.broadcasted_iota(jnp.int32, s.shape, 1)
                s = jnp.where(col <= row, s, -jnp.inf)
            m_prev = m_ref[...]
            m_new = jnp.maximum(m_prev, jnp.max(s, axis=-1, keepdims=True))
            alpha = jnp.exp(m_prev - m_new)
            p = jnp.exp(s - m_new)
            l_ref[...] = alpha * l_ref[...] + jnp.sum(p, axis=-1, keepdims=True)
            a_ref[...] = alpha * a_ref[...] + jnp.dot(p.astype(BF16), v, preferred_element_type=F32)
            m_ref[...] = m_new

    def body(j, carry):
        step(j, False)
        return carry

    lax.fori_loop(0, i, body, 0)
    step(i, True)

    lv = lam_ref[...]
    lam = (jnp.exp(jnp.sum(lv[0:1] * lv[1:2], axis=-1, keepdims=True))
           - jnp.exp(jnp.sum(lv[2:3] * lv[3:4], axis=-1, keepdims=True)) + lam_init)
    o = a1[...] / l1[...] - lam * (a2[...] / l2[...])
    o_ref[...] = (_rms(o, g_ref[...]) * (1.0 - lam_init)).astype(BF16)


def _diff_attention(pa, lam_p, head_g, *, batch, seq, lam_init, blk=512):
    nq = seq // blk
    kern = functools.partial(_diff_attn_kernel, blk=blk, lam_init=lam_init)
    tile = lambda base: pl.BlockSpec((None, None, None, seq, HEAD_W),
                                     lambda b, h, i: (base + h, b, 0, 0, 0))
    return pl.pallas_call(
        kern,
        grid=(batch, N_HEADS, nq),
        in_specs=[
            pl.BlockSpec((4, A_QK_DIM), lambda b, h, i: (0, 0)),
            pl.BlockSpec((1, HEAD_W), lambda b, h, i: (0, 0)),
            pl.BlockSpec((None, None, None, blk, HEAD_W), lambda b, h, i: (h, b, 0, i, 0)),
            tile(N_HEADS), tile(2 * N_HEADS),
        ],
        out_specs=pl.BlockSpec((None, blk, HEAD_W), lambda b, h, i: (h, b * nq + i, 0)),
        out_shape=jax.ShapeDtypeStruct((N_HEADS, batch * seq, HEAD_W), BF16),
        scratch_shapes=[pltpu.VMEM((blk, 1), F32), pltpu.VMEM((blk, 1), F32),
                        pltpu.VMEM((blk, HEAD_W), F32)] * 2,
        compiler_params=_cparams(("parallel", "parallel", "arbitrary")),
        name="diff_attention",
    )(lam_p, head_g, pa, pa, pa)


DIL_SPAN = B_BLOCK * B_PAIRS[-1][1]


def _band_unit(q, kk, vv, prev_ok):
    s = lax.dot_general(q, kk, (((1,), (1,)), ((), ())), preferred_element_type=F32)
    s = s * (HEAD_W ** -0.5)
    qi = lax.broadcasted_iota(jnp.int32, s.shape, 0)
    kj = lax.broadcasted_iota(jnp.int32, s.shape, 1)
    valid = (kj >= qi) & (kj <= qi + B_BLOCK)
    if prev_ok is not None:
        valid = valid & ((kj >= B_BLOCK) | prev_ok)
    s = jnp.where(valid, s, -jnp.inf)
    m = jnp.max(s, axis=-1, keepdims=True)
    p = jnp.exp(s - m)
    den = jnp.sum(p, axis=-1, keepdims=True)
    o = jnp.dot(p.astype(BF16), vv, preferred_element_type=F32) / den
    return o, m + jnp.log(den)


def _dilated_kernel(q1, k1, k1h, v1, v1h, q2, k2, k2h, v2, v2h, q3, k3, k3h, v3, v3h,
                    o_ref, o_scr, lse_scr):
    i = pl.program_id(2)
    prev_ok = i > 0
    cat = lambda a, b: jnp.concatenate([a, b], axis=0)

    def put(g, rows, o, lse):
        o_scr[g, rows, :] = o
        lse_scr[g, rows, :] = jnp.broadcast_to(lse, o.shape)

    o, lse = _band_unit(q1[0:B_BLOCK, :], cat(k1h[...], k1[0:B_BLOCK, :]),
                        cat(v1h[...], v1[0:B_BLOCK, :]), prev_ok)
    put(0, pl.ds(0, B_BLOCK), o, lse)

    def body(n, carry):
        cur = pl.multiple_of(n * B_BLOCK, B_BLOCK)
        prv = pl.multiple_of((n - 1) * B_BLOCK, B_BLOCK)
        o, lse = _band_unit(q1[pl.ds(cur, B_BLOCK), :], k1[pl.ds(prv, 2 * B_BLOCK), :],
                            v1[pl.ds(prv, 2 * B_BLOCK), :], None)
        put(0, pl.ds(cur, B_BLOCK), o, lse)
        return carry

    lax.fori_loop(1, DIL_SPAN // B_BLOCK, body, 0)

    for g, (q, k, kh, v, vh) in ((1, (q2, k2, k2h, v2, v2h)), (2, (q3, k3, k3h, v3, v3h))):
        dil = B_PAIRS[g][1]
        nblk = DIL_SPAN // (dil * B_BLOCK)
        for r in range(dil):
            for ub in range(nblk):
                cur = slice(ub * B_BLOCK, (ub + 1) * B_BLOCK)
                if ub == 0:
                    kk, vv, ok = cat(kh[r], k[r, cur, :]), cat(vh[r], v[r, cur, :]), prev_ok
                else:
                    both = slice((ub - 1) * B_BLOCK, (ub + 1) * B_BLOCK)
                    kk, vv, ok = k[r, both, :], v[r, both, :], None
                o, lse = _band_unit(q[r, cur, :], kk, vv, ok)
                put(g, pl.ds(ub * B_BLOCK * dil + r, B_BLOCK, stride=dil), o, lse)

    chunk = 256
    for c in range(DIL_SPAN // chunk):
        rows = slice(c * chunk, (c + 1) * chunk)
        ls = [lse_scr[g, rows, :] for g in range(3)]
        mx = jnp.maximum(jnp.maximum(ls[0], ls[1]), ls[2])
        es = [jnp.exp(l - mx) for l in ls]
        num = es[0] * o_scr[0, rows, :] + es[1] * o_scr[1, rows, :] + es[2] * o_scr[2, rows, :]
        o_ref[rows, :] = (num / (es[0] + es[1] + es[2])).astype(BF16)


def _dilated_attention(pa, pb, pc, *, batch, seq):
    nspan = seq // DIL_SPAN
    specs = []
    args = []
    for arr, (qb, kb, vb), dil in ((pa, (3 * N_HEADS, 4 * N_HEADS, 5 * N_HEADS), 1),
                                   (pb, (0, N_HEADS, 2 * N_HEADS), B_PAIRS[1][1]),
                                   (pc, (0, N_HEADS, 2 * N_HEADS), B_PAIRS[2][1])):
        span = DIL_SPAN // dil
        per = span // B_BLOCK
        if dil == 1:
            cur = lambda base: pl.BlockSpec((None, None, None, span, HEAD_W),
                                            lambda b, h, i, base=base: (base + h, b, 0, i, 0))
            halo = lambda base, per=per: pl.BlockSpec(
                (None, None, None, B_BLOCK, HEAD_W),
                lambda b, h, i, base=base: (base + h, b, 0, jnp.maximum(i * per - 1, 0), 0))
        else:
            cur = lambda base, dil=dil, span=span: pl.BlockSpec(
                (None, None, dil, span, HEAD_W), lambda b, h, i, base=base: (base + h, b, 0, i, 0))
            halo = lambda base, dil=dil, per=per: pl.BlockSpec(
                (None, None, dil, B_BLOCK, HEAD_W),
                lambda b, h, i, base=base: (base + h, b, 0, jnp.maximum(i * per - 1, 0), 0))
        specs += [cur(qb), cur(kb), halo(kb), cur(vb), halo(vb)]
        args += [arr] * 5
    return pl.pallas_call(
        _dilated_kernel,
        grid=(batch, N_HEADS, nspan),
        in_specs=specs,
        out_specs=pl.BlockSpec((None, DIL_SPAN, HEAD_W), lambda b, h, i: (h, b * nspan + i, 0)),
        out_shape=jax.ShapeDtypeStruct((N_HEADS, batch * seq, HEAD_W), BF16),
        scratch_shapes=[pltpu.VMEM((3, DIL_SPAN, HEAD_W), F32), pltpu.VMEM((3, DIL_SPAN, HEAD_W), F32)],
        compiler_params=_cparams(("parallel", "parallel", "parallel")),
        name="dilated_attention",
    )(*args)


def _merge_kernel(x_ref, oa_ref, ob_ref, ga_ref, gb_ref, wa_ref, wb_ref, wm_ref, g_ref, o_ref):
    heads = lambda ref, n: jnp.concatenate([ref[c] for c in range(n)], axis=-1)
    ya = jnp.dot(heads(oa_ref, N_HEADS), wa_ref[...], preferred_element_type=F32)
    yb = jnp.dot(heads(ob_ref, N_HEADS), wb_ref[...], preferred_element_type=F32)
    ntile = D_MODEL // HEAD_W
    z = heads(ga_ref, ntile).astype(F32) * ya + heads(gb_ref, ntile).astype(F32) * yb
    mix = jnp.dot(z.astype(BF16), wm_ref[...], preferred_element_type=F32)
    o_ref[...] = x_ref[...] + _rms(mix, g_ref[...])


def _merge(xf, oa, ob, pa, wa, wb, wm, g, *, batch, seq, tm=512):
    t = batch * seq
    nrow = seq // tm
    ntile = D_MODEL // HEAD_W
    gate_base = 6 * N_HEADS // ntile
    gate = lambda off: pl.BlockSpec((ntile, None, None, tm, HEAD_W),
                                    lambda i: (gate_base + off, i // nrow, 0, i % nrow, 0))
    full = lambda shape: pl.BlockSpec(shape, lambda i: (0,) * len(shape))
    return pl.pallas_call(
        _merge_kernel,
        grid=(t // tm,),
        in_specs=[
            pl.BlockSpec((tm, D_MODEL), lambda i: (i, 0)),
            pl.BlockSpec((N_HEADS, tm, HEAD_W), lambda i: (0, i, 0)),
            pl.BlockSpec((N_HEADS, tm, HEAD_W), lambda i: (0, i, 0)),
            gate(0), gate(1),
            full((N_HEADS * HEAD_W, D_MODEL)), full((N_HEADS * HEAD_W, D_MODEL)),
            full((D_MODEL, D_MODEL)), full((1, D_MODEL)),
        ],
        out_specs=pl.BlockSpec((tm, D_MODEL), lambda i: (i, 0)),
        out_shape=jax.ShapeDtypeStruct((t, D_MODEL), F32),
        compiler_params=_cparams(("parallel",)),
        name="merge",
    )(xf, oa, ob, pa, pa, wa, wb, wm, g)


FFN_HALO = 16


def _ffn_kernel(x_ref, xh_ref, gpre_ref, wg_ref, wv_ref, cwg_ref, cwv_ref, cbg_ref, cbv_ref,
                wd_ref, gpost_ref, o_ref, h_scr, acc_scr, *, tiles_per_seq):
    i = pl.program_id(0)
    j = pl.program_id(1)

    @pl.when(j == 0)
    def _():
        hh = _rms(xh_ref[...], gpre_ref[...])
        hh = jnp.where(i % tiles_per_seq == 0, jnp.zeros_like(hh), hh)
        h_scr[0:FFN_HALO, :] = hh.astype(BF16)
        h_scr[FFN_HALO:, :] = _rms(x_ref[...], gpre_ref[...]).astype(BF16)
        acc_scr[...] = jnp.zeros(acc_scr.shape, F32)

    def conv(u, cw, cb):
        return (cw[2:3, :] * u[FFN_HALO:, :] + cw[1:2, :] * u[FFN_HALO - 1:-1, :]
                + cw[0:1, :] * u[FFN_HALO - 2:-2, :] + cb)

    h = h_scr[...]
    gate = conv(jnp.dot(h, wg_ref[...], preferred_element_type=F32), cwg_ref[...], cbg_ref[...])
    val = conv(jnp.dot(h, wv_ref[...], preferred_element_type=F32), cwv_ref[...], cbv_ref[...])
    act = (jax.nn.gelu(gate, approximate=True) * val).astype(BF16)
    acc_scr[...] += jnp.dot(act, wd_ref[...], preferred_element_type=F32)

    @pl.when(j == pl.num_programs(1) - 1)
    def _():
        o_ref[...] = x_ref[...] + _rms(acc_scr[...], gpost_ref[...])


def _ffn(xf, gpre, w_up, conv_w, conv_b, w_down, gpost, *, seq, tm=512, fc=1408):
    t = xf.shape[0]
    nf = D_FF // fc
    tiles_per_seq = seq // tm
    hb = tm // FFN_HALO
    kern = functools.partial(_ffn_kernel, tiles_per_seq=tiles_per_seq)
    return pl.pallas_call(
        kern,
        grid=(t // tm, nf),
        in_specs=[
            pl.BlockSpec((tm, D_MODEL), lambda i, j: (i, 0)),
            pl.BlockSpec((FFN_HALO, D_MODEL), lambda i, j: (jnp.maximum(i * hb - 1, 0), 0)),
            pl.BlockSpec((1, D_MODEL), lambda i, j: (0, 0)),
            pl.BlockSpec((D_MODEL, fc), lambda i, j: (0, j)),
            pl.BlockSpec((D_MODEL, fc), lambda i, j: (0, nf + j)),
            pl.BlockSpec((3, fc), lambda i, j: (0, j)),
            pl.BlockSpec((3, fc), lambda i, j: (0, nf + j)),
            pl.BlockSpec((1, fc), lambda i, j: (0, j)),
            pl.BlockSpec((1, fc), lambda i, j: (0, nf + j)),
            pl.BlockSpec((fc, D_MODEL), lambda i, j: (j, 0)),
            pl.BlockSpec((1, D_MODEL), lambda i, j: (0, 0)),
        ],
        out_specs=pl.BlockSpec((tm, D_MODEL), lambda i, j: (i, 0)),
        out_shape=jax.ShapeDtypeStruct((t, D_MODEL), F32),
        scratch_shapes=[pltpu.VMEM((tm + FFN_HALO, D_MODEL), BF16), pltpu.VMEM((tm, D_MODEL), F32)],
        compiler_params=_cparams(("parallel", "arbitrary")),
        name="conv_glu",
    )(xf, xf, gpre, w_up, w_up, conv_w, conv_w, conv_b, conv_b, w_down, gpost)


def kernel(x, positions, pre_mix_g, w_in, diff_lambda, diff_head_g, w_a_out, w_b_out, w_mix_out,
           post_mix_g, pre_ffn_g, w_up, conv_w, conv_b, w_down, post_ffn_g):
    batch, seq, d = x.shape
    assert d == D_MODEL and seq % DIL_SPAN == 0
    depth = w_in.shape[0]
    t = batch * seq
    xf = x.reshape(t, d)
    tabs = _rope_tables(positions)

    a_w = N_HEADS * HEAD_W
    b0 = 3 * a_w
    gsl = lambda part, g: slice(b0 + (3 * part + g) * a_w, b0 + (3 * part + g + 1) * a_w)
    row = lambda v: v.reshape(1, -1)

    for l in range(depth):
        wl = w_in[l]
        w_tok = jnp.concatenate([wl[:, :b0], wl[:, gsl(0, 0)], wl[:, gsl(1, 0)], wl[:, gsl(2, 0)],
                                 wl[:, b0 + 9 * a_w:]], axis=1).astype(BF16)
        w_d4 = jnp.concatenate([wl[:, gsl(p, 1)] for p in range(3)], axis=1).astype(BF16)
        w_d16 = jnp.concatenate([wl[:, gsl(p, 2)] for p in range(3)], axis=1).astype(BF16)
        g_pre = row(pre_mix_g[l])
        pa = _project(xf, g_pre, w_tok, tabs, batch=batch, seq=seq, dil=1, tm=1024,
                      kinds=(K_AQ, K_AK, K_V, K_BQ, K_BK, K_V) + (K_GATE,) * 4)
        pb = _project(xf, g_pre, w_d4, tabs, batch=batch, seq=seq, dil=B_PAIRS[1][1], tm=1024,
                      kinds=(K_BQ, K_BK, K_V))
        pc = _project(xf, g_pre, w_d16, tabs, batch=batch, seq=seq, dil=B_PAIRS[2][1], tm=1024,
                      kinds=(K_BQ, K_BK, K_V))
        lam_init = 0.8 - 0.6 * math.exp(-0.3 * l)
        oa = _diff_attention(pa, diff_lambda[l], row(diff_head_g[l]), batch=batch, seq=seq,
                             lam_init=lam_init)
        ob = _dilated_attention(pa, pb, pc, batch=batch, seq=seq)
        xf = _merge(xf, oa, ob, pa, w_a_out[l].astype(BF16), w_b_out[l].astype(BF16),
                    w_mix_out[l].astype(BF16), row(post_mix_g[l]), batch=batch, seq=seq)
        xf = _ffn(xf, row(pre_ffn_g[l]), w_up[l].astype(BF16), conv_w[l], row(conv_b[l]),
                  w_down[l].astype(BF16), row(post_ffn_g[l]), seq=seq)
    return xf.reshape(batch, seq, d)
```

```python
import functools
import math

import jax
import jax.numpy as jnp
import numpy as np
from jax import lax
from jax.experimental import pallas as pl
from jax.experimental.pallas import tpu as pltpu

F32 = jnp.float32
BF16 = jnp.bfloat16

D_MODEL = 1024
N_HEADS = 4
HEAD_W = 128
A_QK_DIM = 64
B_PAIRS = ((128, 1), (512, 4), (2048, 16))
B_BLOCK = 128
D_FF = 2816
ROPE_THETA = 10000.0
NORM_EPS = 1e-6
COL_TILE = N_HEADS * HEAD_W
VMEM_LIMIT = 56 * 1024 * 1024

K_ROPE_A, K_ROPE_B, K_PLAIN, K_GATE = range(4)


def _cparams(sem):
    return pltpu.CompilerParams(dimension_semantics=sem, vmem_limit_bytes=VMEM_LIMIT)


def _rms(x, g):
    return x * lax.rsqrt(jnp.mean(x * x, axis=-1, keepdims=True) + NORM_EPS) * g


def _rope_table_kernel(pos_ref, f_ref, o_ref, *, npair):
    pos = pos_ref[...].astype(F32)
    for k in range(npair):
        ang = pos * f_ref[2 * k:2 * k + 1, :]
        o_ref[2 * k] = jnp.cos(ang)
        o_ref[2 * k + 1] = jnp.sin(ang) * f_ref[2 * k + 1:2 * k + 2, :]


def _rope_tables(pos_col, with_a):
    t = pos_col.shape[0]
    tm = 1024
    lane = np.arange(HEAD_W)
    sign = jnp.asarray(np.where(lane < HEAD_W // 2, -1.0, 1.0), F32)
    inv_a = ROPE_THETA ** (-jnp.arange(0, A_QK_DIM, 2, dtype=F32) / A_QK_DIM)
    inv_b = ROPE_THETA ** (-jnp.arange(0, HEAD_W, 2, dtype=F32) / HEAD_W)
    rows = ([inv_a[lane % (A_QK_DIM // 2)], sign] if with_a else []) + [inv_b[lane % (HEAD_W // 2)], sign]
    npair = len(rows) // 2
    ftab = jnp.stack(rows + [jnp.zeros((HEAD_W,), F32)] * (8 - len(rows)), axis=0)
    return pl.pallas_call(
        functools.partial(_rope_table_kernel, npair=npair),
        grid=(t // tm,),
        in_specs=[pl.BlockSpec((tm, 1), lambda i: (i, 0)),
                  pl.BlockSpec((8, HEAD_W), lambda i: (0, 0))],
        out_specs=pl.BlockSpec((2 * npair, tm, HEAD_W), lambda i: (0, i, 0)),
        out_shape=jax.ShapeDtypeStruct((2 * npair, t, HEAD_W), F32),
        compiler_params=_cparams(("parallel",)),
        name="rope_tables",
    )(pos_col, ftab)


PROJ_ROWS = 256


def _proj_kernel(x_ref, g_ref, w_ref, tab_ref, o_ref, h_scr, acc_scr, *, dil, kinds, tm):
    j = pl.program_id(2)
    rc = PROJ_ROWS
    n = rc // dil

    @pl.when(j == 0)
    def _():
        h_scr[...] = _rms(x_ref[...], g_ref[...]).astype(BF16)

    def finish(kind, a, cos_sin):
        if kind in (K_ROPE_A, K_ROPE_B):
            cos, sin = cos_sin()
            out = a * cos + pltpu.roll(a, HEAD_W // 2, 1) * sin
        elif kind == K_GATE:
            out = jax.nn.sigmoid(a)
        else:
            out = a
        return out.astype(BF16)

    def run(kind):
        t0 = 2 if (kind == K_ROPE_B and dil == 1) else 0
        for c in range(tm // rc):
            rsl = slice(c * rc, (c + 1) * rc)
            acc = jnp.dot(h_scr[rsl, :], w_ref[...], preferred_element_type=F32)
            for hh in range(N_HEADS):
                a = acc[:, hh * HEAD_W:(hh + 1) * HEAD_W]
                if dil == 1:
                    o_ref[hh, 0, rsl, :] = finish(
                        kind, a, lambda: (tab_ref[t0, rsl, :], tab_ref[t0 + 1, rsl, :]))
                else:
                    acc_scr[hh, rsl, :] = a
            if dil > 1:
                usl = slice(c * n, (c + 1) * n)
                for hh in range(N_HEADS):
                    for r in range(dil):
                        a = acc_scr[hh, pl.ds(c * rc + r, n, stride=dil), :]
                        o_ref[hh, r, usl, :] = finish(
                            kind, a, lambda: (tab_ref[0, r, usl, :], tab_ref[1, r, usl, :]))

    for kind in sorted(set(kinds)):
        cond = functools.reduce(jnp.logical_or, [j == c for c, k in enumerate(kinds) if k == kind])
        pl.when(cond)(functools.partial(run, kind))


def _project(xf, g, w, tabs, *, batch, seq, dil, kinds, tm=1024):
    ncol = len(kinds)
    nrow = seq // tm
    rows = tm // dil
    kern = functools.partial(_proj_kernel, dil=dil, kinds=tuple(kinds), tm=tm)
    if dil == 1:
        tab_spec = pl.BlockSpec((4, tm, HEAD_W), lambda b, i, j: (0, b * nrow + i, 0))
    else:
        tab_spec = pl.BlockSpec((2, None, dil, rows, HEAD_W), lambda b, i, j: (0, b, 0, i, 0))
    return pl.pallas_call(
        kern,
        grid=(batch, nrow, ncol),
        in_specs=[
            pl.BlockSpec((tm, D_MODEL), lambda b, i, j: (b * nrow + i, 0)),
            pl.BlockSpec((1, D_MODEL), lambda b, i, j: (0, 0)),
            pl.BlockSpec((D_MODEL, COL_TILE), lambda b, i, j: (0, j)),
            tab_spec,
        ],
        out_specs=pl.BlockSpec((N_HEADS, None, dil, rows, HEAD_W), lambda b, i, j: (j, b, 0, i, 0)),
        out_shape=jax.ShapeDtypeStruct((ncol * N_HEADS, batch, dil, seq // dil, HEAD_W), BF16),
        scratch_shapes=[pltpu.VMEM((tm, D_MODEL), BF16),
                        pltpu.VMEM((N_HEADS, tm if dil > 1 else 8, HEAD_W), F32)],
        compiler_params=_cparams(("parallel", "parallel", "arbitrary")),
        name=f"in_proj_dil{dil}",
    )(xf, g, w, tabs)


def _diff_attn_kernel(lam_ref, g_ref, q_ref, k_ref, v_ref, o_ref, m_scr, l_scr, acc_scr,
                      *, tq, tk, qs, lam_init):
    i = pl.program_id(2)
    q = q_ref[...]
    comp = (lax.broadcasted_iota(jnp.int32, q.shape, 1) // (A_QK_DIM // 2)) % 2
    zero = jnp.zeros_like(q)
    qc = (jnp.where(comp == 0, q, zero), jnp.where(comp == 1, q, zero))
    m_scr[...] = jnp.full(m_scr.shape, -jnp.inf, F32)
    l_scr[...] = jnp.zeros(l_scr.shape, F32)
    acc_scr[...] = jnp.zeros(acc_scr.shape, F32)
    ones = jnp.ones((tk, HEAD_W), BF16)

    def block(j, d):
        base = pl.multiple_of(j * tk, tk)
        for r in range(tq // qs):
            rows = slice(r * qs, (r + 1) * qs)
            off = 0 if d is None else r * qs - d * tk
            if d is not None and off + qs <= 0:
                continue
            masked = d is not None and off < tk - 1
            w = min(tk, off + qs) if masked else tk
            k = k_ref[pl.ds(base, w), :]
            vext = jnp.concatenate([v_ref[pl.ds(base, w), :], ones[:w]], axis=1)
            for c in range(2):
                s = lax.dot_general(qc[c][rows], k, (((1,), (1,)), ((), ())),
                                    preferred_element_type=F32)
                if masked:
                    row = lax.broadcasted_iota(jnp.int32, s.shape, 0) + off
                    col = lax.broadcasted_iota(jnp.int32, s.shape, 1)
                    s = jnp.where(col <= row, s, -jnp.inf)
                m_prev = m_scr[c, rows, :]
                m_new = jnp.maximum(m_prev, jnp.max(s, axis=-1, keepdims=True))
                p = jnp.exp(s - jnp.tile(m_new, (1, w // HEAD_W)))
                alpha = jnp.exp(m_prev - m_new)
                pv = jnp.dot(p.astype(BF16), vext, preferred_element_type=F32)
                l_scr[c, rows, :] = alpha * l_scr[c, rows, :] + pv[:, HEAD_W:]
                acc_scr[c, rows, :] = alpha * acc_scr[c, rows, :] + pv[:, :HEAD_W]
                m_scr[c, rows, :] = m_new

    def body(j, carry):
        block(j, None)
        return carry

    nd = tq // tk
    lax.fori_loop(0, i * nd, body, 0)
    for d in range(nd):
        block(i * nd + d, d)

    lv = lam_ref[...]
    lam = (jnp.exp(jnp.sum(lv[0:1] * lv[1:2], axis=-1, keepdims=True))
           - jnp.exp(jnp.sum(lv[2:3] * lv[3:4], axis=-1, keepdims=True)) + lam_init)
    for r in range(tq // qs):
        rows = slice(r * qs, (r + 1) * qs)
        o = acc_scr[0, rows, :] / l_scr[0, rows, :] - lam * (acc_scr[1, rows, :] / l_scr[1, rows, :])
        o_ref[rows, :] = (_rms(o, g_ref[...]) * (1.0 - lam_init)).astype(BF16)


def _diff_attention(pa, lam_p, head_g, *, batch, seq, lam_init, tq=2048, tk=512, qs=512):
    nq = seq // tq
    kern = functools.partial(_diff_attn_kernel, tq=tq, tk=tk, qs=qs, lam_init=lam_init)
    tile = lambda base: pl.BlockSpec((None, None, None, seq, HEAD_W),
                                     lambda b, h, i: (base + h, b, 0, 0, 0))
    return pl.pallas_call(
        kern,
        grid=(batch, N_HEADS, nq),
        in_specs=[
            pl.BlockSpec((4, A_QK_DIM), lambda b, h, i: (0, 0)),
            pl.BlockSpec((1, HEAD_W), lambda b, h, i: (0, 0)),
            pl.BlockSpec((None, None, None, tq, HEAD_W), lambda b, h, i: (h, b, 0, i, 0)),
            tile(N_HEADS), tile(2 * N_HEADS),
        ],
        out_specs=pl.BlockSpec((None, tq, HEAD_W), lambda b, h, i: (h, b * nq + i, 0)),
        out_shape=jax.ShapeDtypeStruct((N_HEADS, batch * seq, HEAD_W), BF16),
        scratch_shapes=[pltpu.VMEM((2, tq, HEAD_W), F32)] * 3,
        compiler_params=_cparams(("parallel", "parallel", "arbitrary")),
        name="diff_attention",
    )(lam_p, head_g, pa, pa, pa)


DIL_SPAN = B_BLOCK * B_PAIRS[-1][1]


def _band_bias(prev_ok):
    shape = (B_BLOCK, 2 * B_BLOCK)
    qi = lax.broadcasted_iota(jnp.int32, shape, 0)
    kj = lax.broadcasted_iota(jnp.int32, shape, 1)
    valid = (kj >= qi) & (kj <= qi + B_BLOCK)
    if prev_ok is not None:
        valid = valid & ((kj >= B_BLOCK) | prev_ok)
    return jnp.where(valid, 0.0, -jnp.inf).astype(F32)


def _band_unit(q, kk, vv, bias):
    s = lax.dot_general(q, kk, (((1,), (1,)), ((), ())), preferred_element_type=F32)
    s = s * (HEAD_W ** -0.5) + bias
    m = jnp.max(s, axis=-1, keepdims=True)
    p = jnp.exp(s - m)
    den = jnp.sum(p, axis=-1, keepdims=True)
    o = jnp.dot(p.astype(BF16), vv, preferred_element_type=F32) / den
    return o, m + jnp.log(den)


def _dilated_kernel(q1, k1, k1h, v1, v1h, q2, k2, k2h, v2, v2h, q3, k3, k3h, v3, v3h,
                    o_ref, o_scr, lse_scr):
    i = pl.program_id(2)
    bias_first = _band_bias(i > 0)
    bias_inner = _band_bias(None)
    cat = lambda a, b: jnp.concatenate([a, b], axis=0)

    def put(g, rows, o, lse):
        o_scr[g, rows, :] = o
        lse_scr[g, rows, :] = jnp.broadcast_to(lse, o.shape)

    for n in range(DIL_SPAN // B_BLOCK):
        cur = slice(n * B_BLOCK, (n + 1) * B_BLOCK)
        if n == 0:
            kk, vv, bias = cat(k1h[...], k1[cur, :]), cat(v1h[...], v1[cur, :]), bias_first
        else:
            both = slice((n - 1) * B_BLOCK, (n + 1) * B_BLOCK)
            kk, vv, bias = k1[both, :], v1[both, :], bias_inner
        o, lse = _band_unit(q1[cur, :], kk, vv, bias)
        put(0, cur, o, lse)

    for g, (q, k, kh, v, vh) in ((1, (q2, k2, k2h, v2, v2h)), (2, (q3, k3, k3h, v3, v3h))):
        dil = B_PAIRS[g][1]
        nblk = DIL_SPAN // (dil * B_BLOCK)
        for r in range(dil):
            for ub in range(nblk):
                cur = slice(ub * B_BLOCK, (ub + 1) * B_BLOCK)
                if ub == 0:
                    kk, vv, bias = cat(kh[r], k[r, cur, :]), cat(vh[r], v[r, cur, :]), bias_first
                else:
                    both = slice((ub - 1) * B_BLOCK, (ub + 1) * B_BLOCK)
                    kk, vv, bias = k[r, both, :], v[r, both, :], bias_inner
                o, lse = _band_unit(q[r, cur, :], kk, vv, bias)
                put(g, pl.ds(ub * B_BLOCK * dil + r, B_BLOCK, stride=dil), o, lse)

    chunk = 256
    for c in range(DIL_SPAN // chunk):
        rows = slice(c * chunk, (c + 1) * chunk)
        ls = [lse_scr[g, rows, :] for g in range(3)]
        mx = jnp.maximum(jnp.maximum(ls[0], ls[1]), ls[2])
        es = [jnp.exp(l - mx) for l in ls]
        num = es[0] * o_scr[0, rows, :] + es[1] * o_scr[1, rows, :] + es[2] * o_scr[2, rows, :]
        o_ref[rows, :] = (num / (es[0] + es[1] + es[2])).astype(BF16)


def _dilated_attention(pa, pb, pc, *, batch, seq):
    nspan = seq // DIL_SPAN
    specs = []
    args = []
    for arr, (qb, kb, vb), dil in ((pa, (3 * N_HEADS, 4 * N_HEADS, 5 * N_HEADS), 1),
                                   (pb, (0, N_HEADS, 2 * N_HEADS), B_PAIRS[1][1]),
                                   (pc, (0, N_HEADS, 2 * N_HEADS), B_PAIRS[2][1])):
        span = DIL_SPAN // dil
        per = span // B_BLOCK
        if dil == 1:
            cur = lambda base: pl.BlockSpec((None, None, None, span, HEAD_W),
                                            lambda b, h, i, base=base: (base + h, b, 0, i, 0))
            halo = lambda base, per=per: pl.BlockSpec(
                (None, None, None, B_BLOCK, HEAD_W),
                lambda b, h, i, base=base: (base + h, b, 0, jnp.maximum(i * per - 1, 0), 0))
        else:
            cur = lambda base, dil=dil, span=span: pl.BlockSpec(
                (None, None, dil, span, HEAD_W), lambda b, h, i, base=base: (base + h, b, 0, i, 0))
            halo = lambda base, dil=dil, per=per: pl.BlockSpec(
                (None, None, dil, B_BLOCK, HEAD_W),
                lambda b, h, i, base=base: (base + h, b, 0, jnp.maximum(i * per - 1, 0), 0))
        specs += [cur(qb), cur(kb), halo(kb), cur(vb), halo(vb)]
        args += [arr] * 5
    return pl.pallas_call(
        _dilated_kernel,
        grid=(batch, N_HEADS, nspan),
        in_specs=specs,
        out_specs=pl.BlockSpec((None, DIL_SPAN, HEAD_W), lambda b, h, i: (h, b * nspan + i, 0)),
        out_shape=jax.ShapeDtypeStruct((N_HEADS, batch * seq, HEAD_W), BF16),
        scratch_shapes=[pltpu.VMEM((3, DIL_SPAN, HEAD_W), F32), pltpu.VMEM((3, DIL_SPAN, HEAD_W), F32)],
        compiler_params=_cparams(("parallel", "parallel", "parallel")),
        name="dilated_attention",
    )(*args)


def _merge_kernel(x_ref, oa_ref, ob_ref, ga_ref, gb_ref, wa_ref, wb_ref, wm_ref, g_ref, o_ref):
    heads = lambda ref, n: jnp.concatenate([ref[c] for c in range(n)], axis=-1)
    ya = jnp.dot(heads(oa_ref, N_HEADS), wa_ref[...], preferred_element_type=F32)
    yb = jnp.dot(heads(ob_ref, N_HEADS), wb_ref[...], preferred_element_type=F32)
    ntile = D_MODEL // HEAD_W
    z = heads(ga_ref, ntile).astype(F32) * ya + heads(gb_ref, ntile).astype(F32) * yb
    mix = jnp.dot(z.astype(BF16), wm_ref[...], preferred_element_type=F32)
    o_ref[...] = x_ref[...] + _rms(mix, g_ref[...])


def _merge(xf, oa, ob, pa, wa, wb, wm, g, *, batch, seq, tm=512):
    t = batch * seq
    nrow = seq // tm
    ntile = D_MODEL // HEAD_W
    gate_base = 6 * N_HEADS // ntile
    gate = lambda off: pl.BlockSpec((ntile, None, None, tm, HEAD_W),
                                    lambda i: (gate_base + off, i // nrow, 0, i % nrow, 0))
    full = lambda shape: pl.BlockSpec(shape, lambda i: (0,) * len(shape))
    return pl.pallas_call(
        _merge_kernel,
        grid=(t // tm,),
        in_specs=[
            pl.BlockSpec((tm, D_MODEL), lambda i: (i, 0)),
            pl.BlockSpec((N_HEADS, tm, HEAD_W), lambda i: (0, i, 0)),
            pl.BlockSpec((N_HEADS, tm, HEAD_W), lambda i: (0, i, 0)),
            gate(0), gate(1),
            full((N_HEADS * HEAD_W, D_MODEL)), full((N_HEADS * HEAD_W, D_MODEL)),
            full((D_MODEL, D_MODEL)), full((1, D_MODEL)),
        ],
        out_specs=pl.BlockSpec((tm, D_MODEL), lambda i: (i, 0)),
        out_shape=jax.ShapeDtypeStruct((t, D_MODEL), F32),
        compiler_params=_cparams(("parallel",)),
        name="merge",
    )(xf, oa, ob, pa, pa, wa, wb, wm, g)


FFN_HALO = 16


def _ffn_kernel(x_ref, xh_ref, gpre_ref, wg_ref, wv_ref, cw_ref, cb_ref, wd_ref, gpost_ref, o_ref,
                h_scr, *, tiles_per_seq):
    i = pl.program_id(0)
    hh = _rms(xh_ref[...], gpre_ref[...])
    hh = jnp.where(i % tiles_per_seq == 0, jnp.zeros_like(hh), hh)
    h_scr[0:FFN_HALO, :] = hh.astype(BF16)
    h_scr[FFN_HALO:, :] = _rms(x_ref[...], gpre_ref[...]).astype(BF16)

    def conv(u, cw, cb):
        return (cw[2:3, :] * u[FFN_HALO:, :] + cw[1:2, :] * u[FFN_HALO - 1:-1, :]
                + cw[0:1, :] * u[FFN_HALO - 2:-2, :] + cb)

    h = h_scr[...]
    gate = conv(jnp.dot(h, wg_ref[...], preferred_element_type=F32),
                cw_ref[:, :D_FF], cb_ref[:, :D_FF])
    val = conv(jnp.dot(h, wv_ref[...], preferred_element_type=F32),
               cw_ref[:, D_FF:], cb_ref[:, D_FF:])
    act = (jax.nn.gelu(gate, approximate=True) * val).astype(BF16)
    y = jnp.dot(act, wd_ref[...], preferred_element_type=F32)
    o_ref[...] = x_ref[...] + _rms(y, gpost_ref[...])


def _ffn(xf, gpre, w_up, conv_w, conv_b, w_down, gpost, *, seq, tm=512):
    t = xf.shape[0]
    tiles_per_seq = seq // tm
    hb = tm // FFN_HALO
    kern = functools.partial(_ffn_kernel, tiles_per_seq=tiles_per_seq)
    const = lambda shape, col=0: pl.BlockSpec(shape, lambda i: (0, col),
                                              pipeline_mode=pl.Buffered(1))
    return pl.pallas_call(
        kern,
        grid=(t // tm,),
        in_specs=[
            pl.BlockSpec((tm, D_MODEL), lambda i: (i, 0)),
            pl.BlockSpec((FFN_HALO, D_MODEL), lambda i: (jnp.maximum(i * hb - 1, 0), 0)),
            const((1, D_MODEL)),
            const((D_MODEL, D_FF), 0), const((D_MODEL, D_FF), 1),
            const((3, 2 * D_FF)), const((1, 2 * D_FF)),
            const((D_FF, D_MODEL)), const((1, D_MODEL)),
        ],
        out_specs=pl.BlockSpec((tm, D_MODEL), lambda i: (i, 0)),
        out_shape=jax.ShapeDtypeStruct((t, D_MODEL), F32),
        scratch_shapes=[pltpu.VMEM((tm + FFN_HALO, D_MODEL), BF16)],
        compiler_params=_cparams(("parallel",)),
        name="conv_glu",
    )(xf, xf, gpre, w_up, w_up, conv_w, conv_b, w_down, gpost)


def kernel(x, positions, pre_mix_g, w_in, diff_lambda, diff_head_g, w_a_out, w_b_out, w_mix_out,
           post_mix_g, pre_ffn_g, w_up, conv_w, conv_b, w_down, post_ffn_g):
    batch, seq, d = x.shape
    assert d == D_MODEL and seq % DIL_SPAN == 0
    depth = w_in.shape[0]
    t = batch * seq
    xf = x.reshape(t, d)
    tabs = _rope_tables(positions.reshape(t, 1), with_a=True)
    d4, d16 = B_PAIRS[1][1], B_PAIRS[2][1]
    res_major = lambda dil: positions.reshape(batch, seq // dil, dil).swapaxes(1, 2).reshape(t, 1)
    tabs_d4 = _rope_tables(res_major(d4), with_a=False).reshape(2, batch, d4, seq // d4, HEAD_W)
    tabs_d16 = _rope_tables(res_major(d16), with_a=False).reshape(2, batch, d16, seq // d16, HEAD_W)

    a_w = N_HEADS * HEAD_W
    b0 = 3 * a_w
    gsl = lambda part, g: slice(b0 + (3 * part + g) * a_w, b0 + (3 * part + g + 1) * a_w)
    row = lambda v: v.reshape(1, -1)
    half = A_QK_DIM // 2

    def regroup_a(w):
        return w.reshape(d, N_HEADS, 2, 2, half).swapaxes(2, 3).reshape(d, a_w)

    for l in range(depth):
        wl = w_in[l]
        w_tok = jnp.concatenate([regroup_a(wl[:, :a_w]) * (A_QK_DIM ** -0.5),
                                 regroup_a(wl[:, a_w:2 * a_w]), wl[:, 2 * a_w:b0],
                                 wl[:, gsl(0, 0)], wl[:, gsl(1, 0)], wl[:, gsl(2, 0)],
                                 wl[:, b0 + 9 * a_w:]], axis=1).astype(BF16)
        w_d4 = jnp.concatenate([wl[:, gsl(p, 1)] for p in range(3)], axis=1).astype(BF16)
        w_d16 = jnp.concatenate([wl[:, gsl(p, 2)] for p in range(3)], axis=1).astype(BF16)
        g_pre = row(pre_mix_g[l])
        b_kinds = (K_ROPE_B, K_ROPE_B, K_PLAIN)
        pa = _project(xf, g_pre, w_tok, tabs, batch=batch, seq=seq, dil=1,
                      kinds=(K_ROPE_A, K_ROPE_A, K_PLAIN) + b_kinds + (K_GATE,) * 4)
        pb = _project(xf, g_pre, w_d4, tabs_d4, batch=batch, seq=seq, dil=d4, kinds=b_kinds)
        pc = _project(xf, g_pre, w_d16, tabs_d16, batch=batch, seq=seq, dil=d16, kinds=b_kinds)
        lam_init = 0.8 - 0.6 * math.exp(-0.3 * l)
        oa = _diff_attention(pa, diff_lambda[l], row(diff_head_g[l]), batch=batch, seq=seq,
                             lam_init=lam_init)
        ob = _dilated_attention(pa, pb, pc, batch=batch, seq=seq)
        xf = _merge(xf, oa, ob, pa, w_a_out[l].astype(BF16), w_b_out[l].astype(BF16),
                    w_mix_out[l].astype(BF16), row(post_mix_g[l]), batch=batch, seq=seq)
        xf = _ffn(xf, row(pre_ffn_g[l]), w_up[l].astype(BF16), conv_w[l], row(conv_b[l]),
                  w_down[l].astype(BF16), row(post_ffn_g[l]), seq=seq)
    return xf.reshape(batch, seq, d)
```

```python
import functools
import math

import jax
import jax.numpy as jnp
import numpy as np
from jax import lax
from jax.experimental import pallas as pl
from jax.experimental.pallas import tpu as pltpu

F32 = jnp.float32
BF16 = jnp.bfloat16

D_MODEL = 1024
N_HEADS = 4
HEAD_W = 128
A_QK_DIM = 64
B_PAIRS = ((128, 1), (512, 4), (2048, 16))
B_BLOCK = 128
D_FF = 2816
ROPE_THETA = 10000.0
NORM_EPS = 1e-6
COL_TILE = N_HEADS * HEAD_W
VMEM_LIMIT = 56 * 1024 * 1024

K_ROPE_A, K_ROPE_B, K_PLAIN, K_GATE = range(4)


def _cparams(sem):
    return pltpu.CompilerParams(dimension_semantics=sem, vmem_limit_bytes=VMEM_LIMIT)


def _rms(x, g):
    return x * lax.rsqrt(jnp.mean(x * x, axis=-1, keepdims=True) + NORM_EPS) * g


def _rope_table_kernel(pos_ref, f_ref, o_ref):
    pos = pos_ref[...].astype(F32)
    for k in range(2):
        ang = pos * f_ref[2 * k:2 * k + 1, :]
        o_ref[2 * k] = jnp.cos(ang)
        o_ref[2 * k + 1] = jnp.sin(ang) * f_ref[2 * k + 1:2 * k + 2, :]


def _rope_tables(pos_col):
    t = pos_col.shape[0]
    tm = 1024
    lane = np.arange(HEAD_W)
    sign = jnp.asarray(np.where(lane < HEAD_W // 2, -1.0, 1.0), F32)
    inv_a = ROPE_THETA ** (-jnp.arange(0, A_QK_DIM, 2, dtype=F32) / A_QK_DIM)
    inv_b = ROPE_THETA ** (-jnp.arange(0, HEAD_W, 2, dtype=F32) / HEAD_W)
    rows = [inv_a[lane % (A_QK_DIM // 2)], sign, inv_b[lane % (HEAD_W // 2)], sign]
    ftab = jnp.stack(rows + [jnp.zeros((HEAD_W,), F32)] * (8 - len(rows)), axis=0)
    return pl.pallas_call(
        _rope_table_kernel,
        grid=(t // tm,),
        in_specs=[pl.BlockSpec((tm, 1), lambda i: (i, 0)),
                  pl.BlockSpec((8, HEAD_W), lambda i: (0, 0))],
        out_specs=pl.BlockSpec((4, tm, HEAD_W), lambda i: (0, i, 0)),
        out_shape=jax.ShapeDtypeStruct((4, t, HEAD_W), F32),
        compiler_params=_cparams(("parallel",)),
        name="rope_tables",
    )(pos_col, ftab)


PROJ_ROWS = 256
B_KINDS = (K_ROPE_B, K_ROPE_B, K_PLAIN)
PROJ_TILES = (tuple((k, 0) for k in (K_ROPE_A, K_ROPE_A, K_PLAIN) + B_KINDS + (K_GATE,) * 4)
              + tuple((k, 1) for k in B_KINDS) + tuple((k, 2) for k in B_KINDS))


def _proj_kernel(x_ref, g_ref, w_ref, tab1_ref, tab4_ref, tab16_ref, o1_ref, o4_ref, o16_ref,
                 h_scr, acc_scr, *, tm):
    rc = PROJ_ROWS
    h_scr[...] = _rms(x_ref[...], g_ref[...]).astype(BF16)
    tabs = (tab1_ref, tab4_ref, tab16_ref)
    outs = (o1_ref, o4_ref, o16_ref)

    def finish(kind, a, cos_sin):
        if kind in (K_ROPE_A, K_ROPE_B):
            cos, sin = cos_sin()
            out = a * cos + pltpu.roll(a, HEAD_W // 2, 1) * sin
        elif kind == K_GATE:
            out = jax.nn.sigmoid(a)
        else:
            out = a
        return out.astype(BF16)

    first = {}
    for ct, (kind, grp) in enumerate(PROJ_TILES):
        first.setdefault(grp, ct)
        head0 = (ct - first[grp]) * N_HEADS
        dil = B_PAIRS[grp][1]
        n = rc // dil
        tab_ref, o_ref = tabs[grp], outs[grp]
        t0 = 2 if (kind == K_ROPE_B and grp == 0) else 0
        for c in range(tm // rc):
            rsl = slice(c * rc, (c + 1) * rc)
            acc = jnp.dot(h_scr[rsl, :], w_ref[:, ct * COL_TILE:(ct + 1) * COL_TILE],
                          preferred_element_type=F32)
            slot = (ct + c) % 2
            for hh in range(N_HEADS):
                a = acc[:, hh * HEAD_W:(hh + 1) * HEAD_W]
                if dil == 1:
                    o_ref[head0 + hh, 0, rsl, :] = finish(
                        kind, a, lambda: (tab_ref[t0, rsl, :], tab_ref[t0 + 1, rsl, :]))
                else:
                    acc_scr[slot, hh] = a
            if dil > 1:
                usl = slice(c * n, (c + 1) * n)
                for hh in range(N_HEADS):
                    for r in range(dil):
                        a = acc_scr[slot, hh, pl.ds(r, n, stride=dil), :]
                        o_ref[head0 + hh, r, usl, :] = finish(
                            kind, a, lambda: (tab_ref[0, r, usl, :], tab_ref[1, r, usl, :]))


def _project(xf, g, w, tabs, tabs4, tabs16, *, batch, seq, tm=512):
    nrow = seq // tm
    d4, d16 = B_PAIRS[1][1], B_PAIRS[2][1]
    n1 = sum(1 for _, grp in PROJ_TILES if grp == 0) * N_HEADS
    nd = len(B_KINDS) * N_HEADS
    const = lambda shape: pl.BlockSpec(shape, lambda b, i: (0, 0), pipeline_mode=pl.Buffered(1))
    res = lambda lead, dil: pl.BlockSpec((lead, None, dil, tm // dil, HEAD_W),
                                         lambda b, i: (0, b, 0, i, 0))
    shp = lambda lead, dil: jax.ShapeDtypeStruct((lead, batch, dil, seq // dil, HEAD_W), BF16)
    return pl.pallas_call(
        functools.partial(_proj_kernel, tm=tm),
        grid=(batch, nrow),
        in_specs=[
            pl.BlockSpec((tm, D_MODEL), lambda b, i: (b * nrow + i, 0)),
            const((1, D_MODEL)),
            const((D_MODEL, len(PROJ_TILES) * COL_TILE)),
            pl.BlockSpec((4, tm, HEAD_W), lambda b, i: (0, b * nrow + i, 0)),
            res(2, d4), res(2, d16),
        ],
        out_specs=[res(n1, 1), res(nd, d4), res(nd, d16)],
        out_shape=[shp(n1, 1), shp(nd, d4), shp(nd, d16)],
        scratch_shapes=[pltpu.VMEM((tm, D_MODEL), BF16),
                        pltpu.VMEM((2, N_HEADS, PROJ_ROWS, HEAD_W), F32)],
        compiler_params=_cparams(("parallel", "parallel")),
        name="in_proj",
    )(xf, g, w, tabs, tabs4, tabs16)


def _diff_attn_kernel(lam_ref, g_ref, q_ref, k_ref, v_ref, o_ref, m_scr, l_scr, acc_scr,
                      *, tq, tk, qs):
    i = pl.program_id(2)
    q = q_ref[...]
    comp = (lax.broadcasted_iota(jnp.int32, q.shape, 1) // (A_QK_DIM // 2)) % 2
    zero = jnp.zeros_like(q)
    qc = (jnp.where(comp == 0, q, zero), jnp.where(comp == 1, q, zero))
    m_scr[...] = jnp.full(m_scr.shape, -jnp.inf, F32)
    l_scr[...] = jnp.zeros(l_scr.shape, F32)
    acc_scr[...] = jnp.zeros(acc_scr.shape, F32)
    ones = jnp.ones((tk, HEAD_W), BF16)

    def block(j, d):
        base = pl.multiple_of(j * tk, tk)
        for r in range(tq // qs):
            rows = slice(r * qs, (r + 1) * qs)
            off = 0 if d is None else r * qs - d * tk
            if d is not None and off + qs <= 0:
                continue
            masked = d is not None and off < tk - 1
            w = min(tk, off + qs) if masked else tk
            k = k_ref[pl.ds(base, w), :]
            vext = jnp.concatenate([v_ref[pl.ds(base, w), :], ones[:w]], axis=1)
            for c in range(2):
                s = lax.dot_general(qc[c][rows], k, (((1,), (1,)), ((), ())),
                                    preferred_element_type=F32)
                if masked:
                    row = lax.broadcasted_iota(jnp.int32, s.shape, 0) + off
                    col = lax.broadcasted_iota(jnp.int32, s.shape, 1)
                    s = jnp.where(col <= row, s, -jnp.inf)
                m_prev = m_scr[c, rows, :]
                m_new = jnp.maximum(m_prev, jnp.max(s, axis=-1, keepdims=True))
                p = jnp.exp(s - jnp.tile(m_new, (1, w // HEAD_W)))
                alpha = jnp.exp(m_prev - m_new)
                pv = jnp.dot(p.astype(BF16), vext, preferred_element_type=F32)
                l_scr[c, rows, :] = alpha * l_scr[c, rows, :] + pv[:, HEAD_W:]
                acc_scr[c, rows, :] = alpha * acc_scr[c, rows, :] + pv[:, :HEAD_W]
                m_scr[c, rows, :] = m_new

    def body(j, carry):
        block(j, None)
        return carry

    nd = tq // tk
    lax.fori_loop(0, i * nd, body, 0)
    for d in range(nd):
        block(i * nd + d, d)

    lv = lam_ref[...]
    lam_init = lv[4:5, 0:1]
    lam = (jnp.exp(jnp.sum(lv[0:1] * lv[1:2], axis=-1, keepdims=True))
           - jnp.exp(jnp.sum(lv[2:3] * lv[3:4], axis=-1, keepdims=True)) + lam_init)
    for r in range(tq // qs):
        rows = slice(r * qs, (r + 1) * qs)
        o = acc_scr[0, rows, :] / l_scr[0, rows, :] - lam * (acc_scr[1, rows, :] / l_scr[1, rows, :])
        o_ref[rows, :] = (_rms(o, g_ref[...]) * (1.0 - lam_init)).astype(BF16)


def _diff_attention(pa, lam_p, head_g, *, batch, seq, lam_init, tk=512, qs=512):
    tq = seq
    nq = seq // tq
    kern = functools.partial(_diff_attn_kernel, tq=tq, tk=tk, qs=qs)
    lam_p = jnp.concatenate([lam_p, jnp.full((1, A_QK_DIM), lam_init, F32)], axis=0)
    tile = lambda base: pl.BlockSpec((None, None, None, seq, HEAD_W),
                                     lambda b, h, i: (base + h, b, 0, 0, 0))
    return pl.pallas_call(
        kern,
        grid=(batch, N_HEADS, nq),
        in_specs=[
            pl.BlockSpec((5, A_QK_DIM), lambda b, h, i: (0, 0)),
            pl.BlockSpec((1, HEAD_W), lambda b, h, i: (0, 0)),
            pl.BlockSpec((None, None, None, tq, HEAD_W), lambda b, h, i: (h, b, 0, i, 0)),
            tile(N_HEADS), tile(2 * N_HEADS),
        ],
        out_specs=pl.BlockSpec((None, tq, HEAD_W), lambda b, h, i: (h, b * nq + i, 0)),
        out_shape=jax.ShapeDtypeStruct((N_HEADS, batch * seq, HEAD_W), BF16),
        scratch_shapes=[pltpu.VMEM((2, tq, HEAD_W), F32)] * 3,
        compiler_params=_cparams(("parallel", "parallel", "arbitrary")),
        name="diff_attention",
    )(lam_p, head_g, pa, pa, pa)


DIL_SPAN = B_BLOCK * B_PAIRS[-1][1]


def _band_bias(prev_ok):
    shape = (B_BLOCK, 2 * B_BLOCK)
    qi = lax.broadcasted_iota(jnp.int32, shape, 0)
    kj = lax.broadcasted_iota(jnp.int32, shape, 1)
    valid = (kj >= qi) & (kj <= qi + B_BLOCK)
    if prev_ok is not None:
        valid = valid & ((kj >= B_BLOCK) | prev_ok)
    return jnp.where(valid, 0.0, -jnp.inf).astype(F32)


def _band_unit(q, kk, vv, bias):
    s = lax.dot_general(q, kk, (((1,), (1,)), ((), ())), preferred_element_type=F32)
    s = s * (HEAD_W ** -0.5) + bias
    m = jnp.max(s, axis=-1, keepdims=True)
    p = jnp.exp(s - m)
    den = jnp.sum(p, axis=-1, keepdims=True)
    o = jnp.dot(p.astype(BF16), vv, preferred_element_type=F32) / den
    return o, m + jnp.log(den)


def _dilated_kernel(q1, k1, k1h, v1, v1h, q2, k2, k2h, v2, v2h, q3, k3, k3h, v3, v3h,
                    o_ref, o_scr, lse_scr):
    i = pl.program_id(2)
    bias_first = _band_bias(i > 0)
    bias_inner = _band_bias(None)
    cat = lambda a, b: jnp.concatenate([a, b], axis=0)

    def put(g, rows, o, lse):
        o_scr[g, rows, :] = o
        lse_scr[g, rows, :] = jnp.broadcast_to(lse, o.shape)

    for n in range(DIL_SPAN // B_BLOCK):
        cur = slice(n * B_BLOCK, (n + 1) * B_BLOCK)
        if n == 0:
            kk, vv, bias = cat(k1h[...], k1[cur, :]), cat(v1h[...], v1[cur, :]), bias_first
        else:
            both = slice((n - 1) * B_BLOCK, (n + 1) * B_BLOCK)
            kk, vv, bias = k1[both, :], v1[both, :], bias_inner
        o, lse = _band_unit(q1[cur, :], kk, vv, bias)
        put(0, cur, o, lse)

    for g, (q, k, kh, v, vh) in ((1, (q2, k2, k2h, v2, v2h)), (2, (q3, k3, k3h, v3, v3h))):
        dil = B_PAIRS[g][1]
        nblk = DIL_SPAN // (dil * B_BLOCK)
        for r in range(dil):
            for ub in range(nblk):
                cur = slice(ub * B_BLOCK, (ub + 1) * B_BLOCK)
                if ub == 0:
                    kk, vv, bias = cat(kh[r], k[r, cur, :]), cat(vh[r], v[r, cur, :]), bias_first
                else:
                    both = slice((ub - 1) * B_BLOCK, (ub + 1) * B_BLOCK)
                    kk, vv, bias = k[r, both, :], v[r, both, :], bias_inner
                o, lse = _band_unit(q[r, cur, :], kk, vv, bias)
                put(g, pl.ds(ub * B_BLOCK * dil + r, B_BLOCK, stride=dil), o, lse)

    chunk = 256
    for c in range(DIL_SPAN // chunk):
        rows = slice(c * chunk, (c + 1) * chunk)
        ls = [lse_scr[g, rows, :] for g in range(3)]
        mx = jnp.maximum(jnp.maximum(ls[0], ls[1]), ls[2])
        es = [jnp.exp(l - mx) for l in ls]
        num = es[0] * o_scr[0, rows, :] + es[1] * o_scr[1, rows, :] + es[2] * o_scr[2, rows, :]
        o_ref[rows, :] = (num / (es[0] + es[1] + es[2])).astype(BF16)


def _dilated_attention(pa, pb, pc, *, batch, seq):
    nspan = seq // DIL_SPAN
    specs = []
    args = []
    for arr, (qb, kb, vb), dil in ((pa, (3 * N_HEADS, 4 * N_HEADS, 5 * N_HEADS), 1),
                                   (pb, (0, N_HEADS, 2 * N_HEADS), B_PAIRS[1][1]),
                                   (pc, (0, N_HEADS, 2 * N_HEADS), B_PAIRS[2][1])):
        span = DIL_SPAN // dil
        per = span // B_BLOCK
        if dil == 1:
            cur = lambda base: pl.BlockSpec((None, None, None, span, HEAD_W),
                                            lambda b, h, i, base=base: (base + h, b, 0, i, 0))
            halo = lambda base, per=per: pl.BlockSpec(
                (None, None, None, B_BLOCK, HEAD_W),
                lambda b, h, i, base=base: (base + h, b, 0, jnp.maximum(i * per - 1, 0), 0))
        else:
            cur = lambda base, dil=dil, span=span: pl.BlockSpec(
                (None, None, dil, span, HEAD_W), lambda b, h, i, base=base: (base + h, b, 0, i, 0))
            halo = lambda base, dil=dil, per=per: pl.BlockSpec(
                (None, None, dil, B_BLOCK, HEAD_W),
                lambda b, h, i, base=base: (base + h, b, 0, jnp.maximum(i * per - 1, 0), 0))
        specs += [cur(qb), cur(kb), halo(kb), cur(vb), halo(vb)]
        args += [arr] * 5
    return pl.pallas_call(
        _dilated_kernel,
        grid=(batch, N_HEADS, nspan),
        in_specs=specs,
        out_specs=pl.BlockSpec((None, DIL_SPAN, HEAD_W), lambda b, h, i: (h, b * nspan + i, 0)),
        out_shape=jax.ShapeDtypeStruct((N_HEADS, batch * seq, HEAD_W), BF16),
        scratch_shapes=[pltpu.VMEM((3, DIL_SPAN, HEAD_W), F32), pltpu.VMEM((3, DIL_SPAN, HEAD_W), F32)],
        compiler_params=_cparams(("parallel", "parallel", "parallel")),
        name="dilated_attention",
    )(*args)


def _merge_kernel(x_ref, oa_ref, ob_ref, ga_ref, gb_ref, wa_ref, wb_ref, wm_ref, g_ref, o_ref):
    heads = lambda ref, n: jnp.concatenate([ref[c] for c in range(n)], axis=-1)
    ya = jnp.dot(heads(oa_ref, N_HEADS), wa_ref[...], preferred_element_type=F32)
    yb = jnp.dot(heads(ob_ref, N_HEADS), wb_ref[...], preferred_element_type=F32)
    ntile = D_MODEL // HEAD_W
    z = heads(ga_ref, ntile).astype(F32) * ya + heads(gb_ref, ntile).astype(F32) * yb
    mix = jnp.dot(z.astype(BF16), wm_ref[...], preferred_element_type=F32)
    o_ref[...] = x_ref[...] + _rms(mix, g_ref[...])


def _merge(xf, oa, ob, pa, wa, wb, wm, g, *, batch, seq, tm=512):
    t = batch * seq
    nrow = seq // tm
    ntile = D_MODEL // HEAD_W
    gate_base = 6 * N_HEADS // ntile
    gate = lambda off: pl.BlockSpec((ntile, None, None, tm, HEAD_W),
                                    lambda i: (gate_base + off, i // nrow, 0, i % nrow, 0))
    full = lambda shape: pl.BlockSpec(shape, lambda i: (0,) * len(shape))
    return pl.pallas_call(
        _merge_kernel,
        grid=(t // tm,),
        in_specs=[
            pl.BlockSpec((tm, D_MODEL), lambda i: (i, 0)),
            pl.BlockSpec((N_HEADS, tm, HEAD_W), lambda i: (0, i, 0)),
            pl.BlockSpec((N_HEADS, tm, HEAD_W), lambda i: (0, i, 0)),
            gate(0), gate(1),
            full((N_HEADS * HEAD_W, D_MODEL)), full((N_HEADS * HEAD_W, D_MODEL)),
            full((D_MODEL, D_MODEL)), full((1, D_MODEL)),
        ],
        out_specs=pl.BlockSpec((tm, D_MODEL), lambda i: (i, 0)),
        out_shape=jax.ShapeDtypeStruct((t, D_MODEL), F32),
        compiler_params=_cparams(("parallel",)),
        name="merge",
    )(xf, oa, ob, pa, pa, wa, wb, wm, g)


FFN_HALO = 16


def _ffn_kernel(x_ref, xh_ref, gpre_ref, wg_ref, wv_ref, cw_ref, cb_ref, wd_ref, gpost_ref, o_ref,
                h_scr, *, tiles_per_seq):
    i = pl.program_id(0)
    hh = _rms(xh_ref[...], gpre_ref[...])
    hh = jnp.where(i % tiles_per_seq == 0, jnp.zeros_like(hh), hh)
    h_scr[0:FFN_HALO, :] = hh.astype(BF16)
    h_scr[FFN_HALO:, :] = _rms(x_ref[...], gpre_ref[...]).astype(BF16)

    def conv(u, cw, cb):
        return (cw[2:3, :] * u[FFN_HALO:, :] + cw[1:2, :] * u[FFN_HALO - 1:-1, :]
                + cw[0:1, :] * u[FFN_HALO - 2:-2, :] + cb)

    h = h_scr[...]
    gate = conv(jnp.dot(h, wg_ref[...], preferred_element_type=F32),
                cw_ref[:, :D_FF], cb_ref[:, :D_FF])
    val = conv(jnp.dot(h, wv_ref[...], preferred_element_type=F32),
               cw_ref[:, D_FF:], cb_ref[:, D_FF:])
    act = (jax.nn.gelu(gate, approximate=True) * val).astype(BF16)
    y = jnp.dot(act, wd_ref[...], preferred_element_type=F32)
    o_ref[...] = x_ref[...] + _rms(y, gpost_ref[...])


def _ffn(xf, gpre, w_up, conv_w, conv_b, w_down, gpost, *, seq, tm=512):
    t = xf.shape[0]
    tiles_per_seq = seq // tm
    hb = tm // FFN_HALO
    kern = functools.partial(_ffn_kernel, tiles_per_seq=tiles_per_seq)
    const = lambda shape, col=0: pl.BlockSpec(shape, lambda i: (0, col),
                                              pipeline_mode=pl.Buffered(1))
    return pl.pallas_call(
        kern,
        grid=(t // tm,),
        in_specs=[
            pl.BlockSpec((tm, D_MODEL), lambda i: (i, 0)),
            pl.BlockSpec((FFN_HALO, D_MODEL), lambda i: (jnp.maximum(i * hb - 1, 0), 0)),
            const((1, D_MODEL)),
            const((D_MODEL, D_FF), 0), const((D_MODEL, D_FF), 1),
            const((3, 2 * D_FF)), const((1, 2 * D_FF)),
            const((D_FF, D_MODEL)), const((1, D_MODEL)),
        ],
        out_specs=pl.BlockSpec((tm, D_MODEL), lambda i: (i, 0)),
        out_shape=jax.ShapeDtypeStruct((t, D_MODEL), F32),
        scratch_shapes=[pltpu.VMEM((tm + FFN_HALO, D_MODEL), BF16)],
        compiler_params=_cparams(("parallel",)),
        name="conv_glu",
    )(xf, xf, gpre, w_up, w_up, conv_w, conv_b, w_down, gpost)


def kernel(x, positions, pre_mix_g, w_in, diff_lambda, diff_head_g, w_a_out, w_b_out, w_mix_out,
           post_mix_g, pre_ffn_g, w_up, conv_w, conv_b, w_down, post_ffn_g):
    batch, seq, d = x.shape
    assert d == D_MODEL and seq % DIL_SPAN == 0
    depth = w_in.shape[0]
    t = batch * seq
    xf = x.reshape(t, d)
    tabs = _rope_tables(positions.reshape(t, 1))
    d4, d16 = B_PAIRS[1][1], B_PAIRS[2][1]
    res_major = lambda dil: tabs[2:].reshape(2, batch, seq // dil, dil, HEAD_W).swapaxes(2, 3)
    tabs_d4, tabs_d16 = res_major(d4), res_major(d16)

    a_w = N_HEADS * HEAD_W
    b0 = 3 * a_w
    gsl = lambda part, g: slice(b0 + (3 * part + g) * a_w, b0 + (3 * part + g + 1) * a_w)
    row = lambda v: v.reshape(1, -1)
    half = A_QK_DIM // 2

    def regroup_a(w):
        return w.reshape(d, N_HEADS, 2, 2, half).swapaxes(2, 3).reshape(d, a_w)

    for l in range(depth):
        wl = w_in[l]
        w_all = jnp.concatenate(
            [regroup_a(wl[:, :a_w]) * (A_QK_DIM ** -0.5), regroup_a(wl[:, a_w:2 * a_w]),
             wl[:, 2 * a_w:b0], wl[:, gsl(0, 0)], wl[:, gsl(1, 0)], wl[:, gsl(2, 0)],
             wl[:, b0 + 9 * a_w:]]
            + [wl[:, gsl(p, 1)] for p in range(3)] + [wl[:, gsl(p, 2)] for p in range(3)],
            axis=1).astype(BF16)
        pa, pb, pc = _project(xf, row(pre_mix_g[l]), w_all, tabs, tabs_d4, tabs_d16,
                              batch=batch, seq=seq)
        lam_init = 0.8 - 0.6 * math.exp(-0.3 * l)
        oa = _diff_attention(pa, diff_lambda[l], row(diff_head_g[l]), batch=batch, seq=seq,
                             lam_init=lam_init)
        ob = _dilated_attention(pa, pb, pc, batch=batch, seq=seq)
        xf = _merge(xf, oa, ob, pa, w_a_out[l].astype(BF16), w_b_out[l].astype(BF16),
                    w_mix_out[l].astype(BF16), row(post_mix_g[l]), batch=batch, seq=seq)
        xf = _ffn(xf, row(pre_ffn_g[l]), w_up[l].astype(BF16), conv_w[l], row(conv_b[l]),
                  w_down[l].astype(BF16), row(post_ffn_g[l]), seq=seq)
    return xf.reshape(batch, seq, d)
```

```python
import functools
import math

import jax
import jax.numpy as jnp
import numpy as np
from jax import lax
from jax.experimental import pallas as pl
from jax.experimental.pallas import tpu as pltpu

F32 = jnp.float32
BF16 = jnp.bfloat16

D_MODEL = 1024
N_HEADS = 4
HEAD_W = 128
A_QK_DIM = 64
B_PAIRS = ((128, 1), (512, 4), (2048, 16))
B_BLOCK = 128
D_FF = 2816
ROPE_THETA = 10000.0
NORM_EPS = 1e-6
COL_TILE = N_HEADS * HEAD_W
VMEM_LIMIT = 56 * 1024 * 1024

K_ROPE_A, K_ROPE_B, K_PLAIN, K_GATE = range(4)


def _cparams(sem):
    return pltpu.CompilerParams(dimension_semantics=sem, vmem_limit_bytes=VMEM_LIMIT)


def _rms(x, g):
    return x * lax.rsqrt(jnp.mean(x * x, axis=-1, keepdims=True) + NORM_EPS) * g


def _rope_table_kernel(pos_ref, f_ref, o_ref):
    pos = pos_ref[...].astype(F32)
    for k in range(2):
        ang = pos * f_ref[2 * k:2 * k + 1, :]
        o_ref[2 * k] = jnp.cos(ang)
        o_ref[2 * k + 1] = jnp.sin(ang) * f_ref[2 * k + 1:2 * k + 2, :]


def _rope_tables(pos_col):
    t = pos_col.shape[0]
    tm = 1024
    lane = np.arange(HEAD_W)
    sign = jnp.asarray(np.where(lane < HEAD_W // 2, -1.0, 1.0), F32)
    inv_a = ROPE_THETA ** (-jnp.arange(0, A_QK_DIM, 2, dtype=F32) / A_QK_DIM)
    inv_b = ROPE_THETA ** (-jnp.arange(0, HEAD_W, 2, dtype=F32) / HEAD_W)
    rows = [inv_a[lane % (A_QK_DIM // 2)], sign, inv_b[lane % (HEAD_W // 2)], sign]
    ftab = jnp.stack(rows + [jnp.zeros((HEAD_W,), F32)] * (8 - len(rows)), axis=0)
    return pl.pallas_call(
        _rope_table_kernel,
        grid=(t // tm,),
        in_specs=[pl.BlockSpec((tm, 1), lambda i: (i, 0)),
                  pl.BlockSpec((8, HEAD_W), lambda i: (0, 0))],
        out_specs=pl.BlockSpec((4, tm, HEAD_W), lambda i: (0, i, 0)),
        out_shape=jax.ShapeDtypeStruct((4, t, HEAD_W), F32),
        compiler_params=_cparams(("parallel",)),
        name="rope_tables",
    )(pos_col, ftab)


PROJ_ROWS = 256
B_KINDS = (K_ROPE_B, K_ROPE_B, K_PLAIN)
PROJ_TILES = (tuple((k, 0) for k in (K_ROPE_A, K_ROPE_A, K_PLAIN) + B_KINDS + (K_GATE,) * 4)
              + tuple((k, 1) for k in B_KINDS) + tuple((k, 2) for k in B_KINDS))


def _proj_kernel(x_ref, g_ref, w_ref, tab1_ref, tab4_ref, tab16_ref, o1_ref, o4_ref, o16_ref,
                 h_scr, acc_scr, *, tm):
    rc = PROJ_ROWS
    h_scr[...] = _rms(x_ref[...], g_ref[...]).astype(BF16)
    tabs = (tab1_ref, tab4_ref, tab16_ref)
    outs = (o1_ref, o4_ref, o16_ref)

    def finish(kind, a, cos_sin):
        if kind in (K_ROPE_A, K_ROPE_B):
            cos, sin = cos_sin()
            out = a * cos + pltpu.roll(a, HEAD_W // 2, 1) * sin
        elif kind == K_GATE:
            out = jax.nn.sigmoid(a)
        else:
            out = a
        return out.astype(BF16)

    first = {}
    for ct, (kind, grp) in enumerate(PROJ_TILES):
        first.setdefault(grp, ct)
        head0 = (ct - first[grp]) * N_HEADS
        dil = B_PAIRS[grp][1]
        n = rc // dil
        tab_ref, o_ref = tabs[grp], outs[grp]
        t0 = 2 if (kind == K_ROPE_B and grp == 0) else 0
        for c in range(tm // rc):
            rsl = slice(c * rc, (c + 1) * rc)
            acc = jnp.dot(h_scr[rsl, :], w_ref[:, ct * COL_TILE:(ct + 1) * COL_TILE],
                          preferred_element_type=F32)
            slot = (ct + c) % 2
            for hh in range(N_HEADS):
                a = acc[:, hh * HEAD_W:(hh + 1) * HEAD_W]
                if dil == 1:
                    o_ref[head0 + hh, 0, rsl, :] = finish(
                        kind, a, lambda: (tab_ref[t0, rsl, :], tab_ref[t0 + 1, rsl, :]))
                else:
                    acc_scr[slot, hh] = a
            if dil > 1:
                usl = slice(c * n, (c + 1) * n)
                for hh in range(N_HEADS):
                    for r in range(dil):
                        a = acc_scr[slot, hh, pl.ds(r, n, stride=dil), :]
                        o_ref[head0 + hh, r, usl, :] = finish(
                            kind, a, lambda: (tab_ref[0, r, usl, :], tab_ref[1, r, usl, :]))


def _project(xf, g, w, tabs, tabs4, tabs16, *, batch, seq, tm=512):
    nrow = seq // tm
    d4, d16 = B_PAIRS[1][1], B_PAIRS[2][1]
    n1 = sum(1 for _, grp in PROJ_TILES if grp == 0) * N_HEADS
    nd = len(B_KINDS) * N_HEADS
    const = lambda shape: pl.BlockSpec(shape, lambda b, i: (0, 0), pipeline_mode=pl.Buffered(1))
    res = lambda lead, dil: pl.BlockSpec((lead, None, dil, tm // dil, HEAD_W),
                                         lambda b, i: (0, b, 0, i, 0))
    shp = lambda lead, dil: jax.ShapeDtypeStruct((lead, batch, dil, seq // dil, HEAD_W), BF16)
    return pl.pallas_call(
        functools.partial(_proj_kernel, tm=tm),
        grid=(batch, nrow),
        in_specs=[
            pl.BlockSpec((tm, D_MODEL), lambda b, i: (b * nrow + i, 0)),
            const((1, D_MODEL)),
            const((D_MODEL, len(PROJ_TILES) * COL_TILE)),
            pl.BlockSpec((4, tm, HEAD_W), lambda b, i: (0, b * nrow + i, 0)),
            res(2, d4), res(2, d16),
        ],
        out_specs=[res(n1, 1), res(nd, d4), res(nd, d16)],
        out_shape=[shp(n1, 1), shp(nd, d4), shp(nd, d16)],
        scratch_shapes=[pltpu.VMEM((tm, D_MODEL), BF16),
                        pltpu.VMEM((2, N_HEADS, PROJ_ROWS, HEAD_W), F32)],
        compiler_params=_cparams(("parallel", "parallel")),
        name="in_proj",
    )(xf, g, w, tabs, tabs4, tabs16)


def _diff_attn_body(lam_ref, g_ref, q_ref, k_ref, v_ref, o_ref, m_scr, l_scr, acc_scr, *, tk, qs):
    tq = q_ref.shape[0]
    q = q_ref[...]
    comp = (lax.broadcasted_iota(jnp.int32, q.shape, 1) // (A_QK_DIM // 2)) % 2
    zero = jnp.zeros_like(q)
    qc = (jnp.where(comp == 0, q, zero), jnp.where(comp == 1, q, zero))
    m_scr[...] = jnp.full(m_scr.shape, -jnp.inf, F32)
    l_scr[...] = jnp.zeros(l_scr.shape, F32)
    acc_scr[...] = jnp.zeros(acc_scr.shape, F32)
    ones = jnp.ones((tk, HEAD_W), BF16)

    def block(j, d):
        base = j * tk
        for r in range(tq // qs):
            rows = slice(r * qs, (r + 1) * qs)
            off = 0 if d is None else r * qs - d * tk
            if d is not None and off + qs <= 0:
                continue
            masked = d is not None and off < tk - 1
            w = min(tk, off + qs) if masked else tk
            k = k_ref[pl.ds(base, w), :]
            vext = jnp.concatenate([v_ref[pl.ds(base, w), :], ones[:w]], axis=1)
            for c in range(2):
                s = lax.dot_general(qc[c][rows], k, (((1,), (1,)), ((), ())),
                                    preferred_element_type=F32)
                if masked:
                    row = lax.broadcasted_iota(jnp.int32, s.shape, 0) + off
                    col = lax.broadcasted_iota(jnp.int32, s.shape, 1)
                    s = jnp.where(col <= row, s, -jnp.inf)
                m_prev = m_scr[c, rows, :]
                m_new = jnp.maximum(m_prev, jnp.max(s, axis=-1, keepdims=True))
                p = jnp.exp(s - jnp.tile(m_new, (1, w // HEAD_W)))
                alpha = jnp.exp(m_prev - m_new)
                pv = jnp.dot(p.astype(BF16), vext, preferred_element_type=F32)
                l_scr[c, rows, :] = alpha * l_scr[c, rows, :] + pv[:, HEAD_W:]
                acc_scr[c, rows, :] = alpha * acc_scr[c, rows, :] + pv[:, :HEAD_W]
                m_scr[c, rows, :] = m_new

    for d in range(tq // tk):
        block(d, d)

    lv = lam_ref[...]
    lam_init = lv[4:5, 0:1]
    lam = (jnp.exp(jnp.sum(lv[0:1] * lv[1:2], axis=-1, keepdims=True))
           - jnp.exp(jnp.sum(lv[2:3] * lv[3:4], axis=-1, keepdims=True)) + lam_init)
    for r in range(tq // qs):
        rows = slice(r * qs, (r + 1) * qs)
        o = acc_scr[0, rows, :] / l_scr[0, rows, :] - lam * (acc_scr[1, rows, :] / l_scr[1, rows, :])
        o_ref[rows, :] = (_rms(o, g_ref[...]) * (1.0 - lam_init)).astype(BF16)


DIL_SPAN = B_BLOCK * B_PAIRS[-1][1]


def _band_bias(has_prev):
    shape = (B_BLOCK, 2 * B_BLOCK)
    qi = lax.broadcasted_iota(jnp.int32, shape, 0)
    kj = lax.broadcasted_iota(jnp.int32, shape, 1)
    valid = (kj >= qi) & (kj <= qi + B_BLOCK)
    if not has_prev:
        valid = valid & (kj >= B_BLOCK)
    return jnp.where(valid, 0.0, -jnp.inf).astype(F32)


def _band_unit(q, kk, vv, bias):
    s = lax.dot_general(q, kk, (((1,), (1,)), ((), ())), preferred_element_type=F32) + bias
    m = jnp.max(s, axis=-1, keepdims=True)
    scale = HEAD_W ** -0.5
    p = jnp.exp2((s - m) * (scale * math.log2(math.e)))
    ones = jnp.ones((2 * B_BLOCK, HEAD_W), BF16)
    pv = jnp.dot(p.astype(BF16), jnp.concatenate([vv, ones], axis=1), preferred_element_type=F32)
    den = pv[:, HEAD_W:]
    return pv[:, :HEAD_W] / den, m * scale + jnp.log(den)


def _dilated_body(groups, o_ref, o_scr, lse_scr):
    seq = o_ref.shape[0]
    bias = {False: _band_bias(False), True: _band_bias(True)}

    for span in range(seq // DIL_SPAN):
        for g, (q, k, v) in enumerate(groups):
            dil = B_PAIRS[g][1]
            nblk = DIL_SPAN // (dil * B_BLOCK)
            for r in range(dil):
                at = (lambda ref, sl: ref[sl, :]) if g == 0 else (lambda ref, sl, r=r: ref[r, sl, :])
                for ub in range(nblk):
                    n = span * nblk + ub
                    cur = slice(n * B_BLOCK, (n + 1) * B_BLOCK)
                    if n == 0:
                        kk = jnp.concatenate([at(k, cur)] * 2, axis=0)
                        vv = jnp.concatenate([at(v, cur)] * 2, axis=0)
                    else:
                        both = slice((n - 1) * B_BLOCK, (n + 1) * B_BLOCK)
                        kk, vv = at(k, both), at(v, both)
                    o, lse = _band_unit(at(q, cur), kk, vv, bias[n > 0])
                    rows = pl.ds(ub * B_BLOCK * dil + r, B_BLOCK, stride=dil) if dil > 1 else \
                        slice(ub * B_BLOCK, (ub + 1) * B_BLOCK)
                    o_scr[g, rows, :] = o
                    lse_scr[g, rows, :] = lse

        chunk = 256
        for c in range(DIL_SPAN // chunk):
            rows = slice(c * chunk, (c + 1) * chunk)
            ls = [lse_scr[g, rows, :] for g in range(3)]
            mx = jnp.maximum(jnp.maximum(ls[0], ls[1]), ls[2])
            es = [jnp.exp(l - mx) for l in ls]
            num = es[0] * o_scr[0, rows, :] + es[1] * o_scr[1, rows, :] + es[2] * o_scr[2, rows, :]
            out_rows = slice(span * DIL_SPAN + c * chunk, span * DIL_SPAN + (c + 1) * chunk)
            o_ref[out_rows, :] = (num / (es[0] + es[1] + es[2])).astype(BF16)


def _diff_attn_kernel(lam_ref, g_ref, qa, ka, va, oa_ref, m_scr, l_scr, acc_scr):
    _diff_attn_body(lam_ref, g_ref, qa, ka, va, oa_ref, m_scr, l_scr, acc_scr, tk=512, qs=512)


def _dilated_kernel(q1, k1, v1, q2, k2, v2, q3, k3, v3, ob_ref, o_scr, lse_scr):
    _dilated_body(((q1, k1, v1), (q2, k2, v2), (q3, k3, v3)), ob_ref, o_scr, lse_scr)


def _token_mixers(pa, pb, pc, lam_p, head_g, *, batch, seq, lam_init):
    lam_p = jnp.concatenate([lam_p, jnp.full((1, A_QK_DIM), lam_init, F32)], axis=0)

    def qkv_specs(bases, dil):
        shape = (None, None, None, seq, HEAD_W) if dil == 1 else (None, None, dil, seq // dil, HEAD_W)
        return [pl.BlockSpec(shape, lambda b, h, base=base: (base + h, b, 0, 0, 0)) for base in bases]

    out_spec = pl.BlockSpec((None, seq, HEAD_W), lambda b, h: (h, b, 0))
    out_shape = jax.ShapeDtypeStruct((N_HEADS, batch * seq, HEAD_W), BF16)
    a_bases = (0, N_HEADS, 2 * N_HEADS)
    g0_bases = (3 * N_HEADS, 4 * N_HEADS, 5 * N_HEADS)
    oa = pl.pallas_call(
        _diff_attn_kernel,
        grid=(batch, N_HEADS),
        in_specs=[pl.BlockSpec((5, A_QK_DIM), lambda b, h: (0, 0)),
                  pl.BlockSpec((1, HEAD_W), lambda b, h: (0, 0))] + qkv_specs(a_bases, 1),
        out_specs=out_spec,
        out_shape=out_shape,
        scratch_shapes=[pltpu.VMEM((2, seq, HEAD_W), F32)] * 3,
        compiler_params=_cparams(("parallel", "parallel")),
        name="diff_attention",
    )(lam_p, head_g, pa, pa, pa)
    ob = pl.pallas_call(
        _dilated_kernel,
        grid=(batch, N_HEADS),
        in_specs=(qkv_specs(g0_bases, 1) + qkv_specs(a_bases, B_PAIRS[1][1])
                  + qkv_specs(a_bases, B_PAIRS[2][1])),
        out_specs=out_spec,
        out_shape=out_shape,
        scratch_shapes=[pltpu.VMEM((3, DIL_SPAN, HEAD_W), F32)] * 2,
        compiler_params=_cparams(("parallel", "parallel")),
        name="dilated_attention",
    )(pa, pa, pa, pb, pb, pb, pc, pc, pc)
    return oa, ob


def _merge_kernel(x_ref, oa_ref, ob_ref, ga_ref, gb_ref, wa_ref, wb_ref, wm_ref, g_ref, o_ref):
    heads = lambda ref, n: jnp.concatenate([ref[c] for c in range(n)], axis=-1)
    ya = jnp.dot(heads(oa_ref, N_HEADS), wa_ref[...], preferred_element_type=F32)
    yb = jnp.dot(heads(ob_ref, N_HEADS), wb_ref[...], preferred_element_type=F32)
    ntile = D_MODEL // HEAD_W
    z = heads(ga_ref, ntile).astype(F32) * ya + heads(gb_ref, ntile).astype(F32) * yb
    mix = jnp.dot(z.astype(BF16), wm_ref[...], preferred_element_type=F32)
    o_ref[...] = x_ref[...] + _rms(mix, g_ref[...])


def _merge(xf, oa, ob, pa, wa, wb, wm, g, *, batch, seq, tm=512):
    t = batch * seq
    nrow = seq // tm
    ntile = D_MODEL // HEAD_W
    gate_base = 6 * N_HEADS // ntile
    gate = lambda off: pl.BlockSpec((ntile, None, None, tm, HEAD_W),
                                    lambda i: (gate_base + off, i // nrow, 0, i % nrow, 0))
    full = lambda shape: pl.BlockSpec(shape, lambda i: (0,) * len(shape))
    return pl.pallas_call(
        _merge_kernel,
        grid=(t // tm,),
        in_specs=[
            pl.BlockSpec((tm, D_MODEL), lambda i: (i, 0)),
            pl.BlockSpec((N_HEADS, tm, HEAD_W), lambda i: (0, i, 0)),
            pl.BlockSpec((N_HEADS, tm, HEAD_W), lambda i: (0, i, 0)),
            gate(0), gate(1),
            full((N_HEADS * HEAD_W, D_MODEL)), full((N_HEADS * HEAD_W, D_MODEL)),
            full((D_MODEL, D_MODEL)), full((1, D_MODEL)),
        ],
        out_specs=pl.BlockSpec((tm, D_MODEL), lambda i: (i, 0)),
        out_shape=jax.ShapeDtypeStruct((t, D_MODEL), F32),
        compiler_params=_cparams(("parallel",)),
        name="merge",
    )(xf, oa, ob, pa, pa, wa, wb, wm, g)


FFN_HALO = 16


def _ffn_kernel(x_ref, xh_ref, gpre_ref, wg_ref, wv_ref, cw_ref, cb_ref, wd_ref, gpost_ref, o_ref,
                h_scr, *, tiles_per_seq):
    i = pl.program_id(0)
    hh = _rms(xh_ref[...], gpre_ref[...])
    hh = jnp.where(i % tiles_per_seq == 0, jnp.zeros_like(hh), hh)
    h_scr[0:FFN_HALO, :] = hh.astype(BF16)
    h_scr[FFN_HALO:, :] = _rms(x_ref[...], gpre_ref[...]).astype(BF16)

    def conv(u, cw, cb):
        u1 = pltpu.roll(u, 1, 0)[FFN_HALO:, :]
        u2 = pltpu.roll(u, 2, 0)[FFN_HALO:, :]
        return cw[2:3, :] * u[FFN_HALO:, :] + cw[1:2, :] * u1 + cw[0:1, :] * u2 + cb

    h = h_scr[...]
    gate = conv(jnp.dot(h, wg_ref[...], preferred_element_type=F32),
                cw_ref[:, :D_FF], cb_ref[:, :D_FF])
    half_val = conv(jnp.dot(h, wv_ref[...], preferred_element_type=F32),
                    0.5 * cw_ref[:, D_FF:], 0.5 * cb_ref[:, D_FF:])
    c = math.sqrt(2.0 / math.pi)
    inner = gate * (c + (c * 0.044715) * (gate * gate))
    act = ((gate * half_val) * (1.0 + jnp.tanh(inner))).astype(BF16)
    y = jnp.dot(act, wd_ref[...], preferred_element_type=F32)
    o_ref[...] = x_ref[...] + _rms(y, gpost_ref[...])


def _ffn(xf, gpre, w_up, conv_w, conv_b, w_down, gpost, *, seq, tm=512):
    t = xf.shape[0]
    tiles_per_seq = seq // tm
    hb = tm // FFN_HALO
    kern = functools.partial(_ffn_kernel, tiles_per_seq=tiles_per_seq)
    const = lambda shape, col=0: pl.BlockSpec(shape, lambda i: (0, col),
                                              pipeline_mode=pl.Buffered(1))
    return pl.pallas_call(
        kern,
        grid=(t // tm,),
        in_specs=[
            pl.BlockSpec((tm, D_MODEL), lambda i: (i, 0)),
            pl.BlockSpec((FFN_HALO, D_MODEL), lambda i: (jnp.maximum(i * hb - 1, 0), 0)),
            const((1, D_MODEL)),
            const((D_MODEL, D_FF), 0), const((D_MODEL, D_FF), 1),
            const((3, 2 * D_FF)), const((1, 2 * D_FF)),
            const((D_FF, D_MODEL)), const((1, D_MODEL)),
        ],
        out_specs=pl.BlockSpec((tm, D_MODEL), lambda i: (i, 0)),
        out_shape=jax.ShapeDtypeStruct((t, D_MODEL), F32),
        scratch_shapes=[pltpu.VMEM((tm + FFN_HALO, D_MODEL), BF16)],
        compiler_params=_cparams(("parallel",)),
        name="conv_glu",
    )(xf, xf, gpre, w_up, w_up, conv_w, conv_b, w_down, gpost)


def kernel(x, positions, pre_mix_g, w_in, diff_lambda, diff_head_g, w_a_out, w_b_out, w_mix_out,
           post_mix_g, pre_ffn_g, w_up, conv_w, conv_b, w_down, post_ffn_g):
    batch, seq, d = x.shape
    assert d == D_MODEL and seq % DIL_SPAN == 0
    depth = w_in.shape[0]
    t = batch * seq
    xf = x.reshape(t, d)
    tabs = _rope_tables(positions.reshape(t, 1))
    d4, d16 = B_PAIRS[1][1], B_PAIRS[2][1]
    res_major = lambda dil: tabs[2:].reshape(2, batch, seq // dil, dil, HEAD_W).swapaxes(2, 3)
    tabs_d4, tabs_d16 = res_major(d4), res_major(d16)

    a_w = N_HEADS * HEAD_W
    b0 = 3 * a_w
    gsl = lambda part, g: slice(b0 + (3 * part + g) * a_w, b0 + (3 * part + g + 1) * a_w)
    row = lambda v: v.reshape(1, -1)
    half = A_QK_DIM // 2

    def regroup_a(w):
        return w.reshape(d, N_HEADS, 2, 2, half).swapaxes(2, 3).reshape(d, a_w)

    for l in range(depth):
        wl = w_in[l]
        w_all = jnp.concatenate(
            [regroup_a(wl[:, :a_w]) * (A_QK_DIM ** -0.5), regroup_a(wl[:, a_w:2 * a_w]),
             wl[:, 2 * a_w:b0], wl[:, gsl(0, 0)], wl[:, gsl(1, 0)], wl[:, gsl(2, 0)],
             wl[:, b0 + 9 * a_w:]]
            + [wl[:, gsl(p, 1)] for p in range(3)] + [wl[:, gsl(p, 2)] for p in range(3)],
            axis=1).astype(BF16)
        pa, pb, pc = _project(xf, row(pre_mix_g[l]), w_all, tabs, tabs_d4, tabs_d16,
                              batch=batch, seq=seq)
        lam_init = 0.8 - 0.6 * math.exp(-0.3 * l)
        oa, ob = _token_mixers(pa, pb, pc, diff_lambda[l], row(diff_head_g[l]), batch=batch, seq=seq,
                               lam_init=lam_init)
        xf = _merge(xf, oa, ob, pa, w_a_out[l].astype(BF16), w_b_out[l].astype(BF16),
                    w_mix_out[l].astype(BF16), row(post_mix_g[l]), batch=batch, seq=seq)
        xf = _ffn(xf, row(pre_ffn_g[l]), w_up[l].astype(BF16), conv_w[l], row(conv_b[l]),
                  w_down[l].astype(BF16), row(post_ffn_g[l]), seq=seq)
    return xf.reshape(batch, seq, d)
```

```python
import functools
import math

import jax
import jax.numpy as jnp
import numpy as np
from jax import lax
from jax.experimental import pallas as pl
from jax.experimental.pallas import tpu as pltpu

F32 = jnp.float32
BF16 = jnp.bfloat16

D_MODEL = 1024
N_HEADS = 4
HEAD_W = 128
A_QK_DIM = 64
B_PAIRS = ((128, 1), (512, 4), (2048, 16))
B_BLOCK = 128
D_FF = 2816
ROPE_THETA = 10000.0
NORM_EPS = 1e-6
COL_TILE = N_HEADS * HEAD_W
VMEM_LIMIT = 56 * 1024 * 1024

K_ROPE_A, K_ROPE_B, K_PLAIN, K_GATE = range(4)


def _cparams(sem):
    return pltpu.CompilerParams(dimension_semantics=sem, vmem_limit_bytes=VMEM_LIMIT)


def _rms(x, g):
    return x * lax.rsqrt(jnp.mean(x * x, axis=-1, keepdims=True) + NORM_EPS) * g


def _rope_table_kernel(pos_ref, f_ref, o_ref):
    pos = pos_ref[...].astype(F32)
    for k in range(2):
        ang = pos * f_ref[2 * k:2 * k + 1, :]
        o_ref[2 * k] = jnp.cos(ang)
        o_ref[2 * k + 1] = jnp.sin(ang) * f_ref[2 * k + 1:2 * k + 2, :]


def _rope_tables(pos_col):
    t = pos_col.shape[0]
    tm = 1024
    lane = np.arange(HEAD_W)
    sign = jnp.asarray(np.where(lane < HEAD_W // 2, -1.0, 1.0), F32)
    inv_a = ROPE_THETA ** (-jnp.arange(0, A_QK_DIM, 2, dtype=F32) / A_QK_DIM)
    inv_b = ROPE_THETA ** (-jnp.arange(0, HEAD_W, 2, dtype=F32) / HEAD_W)
    rows = [inv_a[lane % (A_QK_DIM // 2)], sign, inv_b[lane % (HEAD_W // 2)], sign]
    ftab = jnp.stack(rows + [jnp.zeros((HEAD_W,), F32)] * (8 - len(rows)), axis=0)
    return pl.pallas_call(
        _rope_table_kernel,
        grid=(t // tm,),
        in_specs=[pl.BlockSpec((tm, 1), lambda i: (i, 0)),
                  pl.BlockSpec((8, HEAD_W), lambda i: (0, 0))],
        out_specs=pl.BlockSpec((4, tm, HEAD_W), lambda i: (0, i, 0)),
        out_shape=jax.ShapeDtypeStruct((4, t, HEAD_W), F32),
        compiler_params=_cparams(("parallel",)),
        name="rope_tables",
    )(pos_col, ftab)


PROJ_ROWS = 256
B_KINDS = (K_ROPE_B, K_ROPE_B, K_PLAIN)
PROJ_TILES = (tuple((k, 0) for k in (K_ROPE_A, K_ROPE_A, K_PLAIN) + B_KINDS + (K_GATE,) * 4)
              + tuple((k, 1) for k in B_KINDS) + tuple((k, 2) for k in B_KINDS))


def _proj_kernel(layer_ref, x_ref, g_ref, w_ref, tab1_ref, tab4_ref, tab16_ref,
                 o1_ref, o4_ref, o16_ref, h_scr, acc_scr, *, tm):
    rc = PROJ_ROWS
    h_scr[...] = _rms(x_ref[...], g_ref[...]).astype(BF16)
    tabs = (tab1_ref, tab4_ref, tab16_ref)
    outs = (o1_ref, o4_ref, o16_ref)

    def finish(kind, a, cos_sin):
        if kind in (K_ROPE_A, K_ROPE_B):
            cos, sin = cos_sin()
            out = a * cos + pltpu.roll(a, HEAD_W // 2, 1) * sin
        elif kind == K_GATE:
            out = jax.nn.sigmoid(a)
        else:
            out = a
        return out.astype(BF16)

    first = {}
    for ct, (kind, grp) in enumerate(PROJ_TILES):
        first.setdefault(grp, ct)
        head0 = (ct - first[grp]) * N_HEADS
        dil = B_PAIRS[grp][1]
        n = rc // dil
        tab_ref, o_ref = tabs[grp], outs[grp]
        t0 = 2 if (kind == K_ROPE_B and grp == 0) else 0
        for c in range(tm // rc):
            rsl = slice(c * rc, (c + 1) * rc)
            acc = jnp.dot(h_scr[rsl, :], w_ref[:, ct * COL_TILE:(ct + 1) * COL_TILE],
                          preferred_element_type=F32)
            slot = (ct + c) % 2
            for hh in range(N_HEADS):
                a = acc[:, hh * HEAD_W:(hh + 1) * HEAD_W]
                if dil == 1:
                    o_ref[head0 + hh, 0, rsl, :] = finish(
                        kind, a, lambda: (tab_ref[t0, rsl, :], tab_ref[t0 + 1, rsl, :]))
                else:
                    acc_scr[slot, hh] = a
            if dil > 1:
                usl = slice(c * n, (c + 1) * n)
                for hh in range(N_HEADS):
                    for r in range(dil):
                        a = acc_scr[slot, hh, pl.ds(r, n, stride=dil), :]
                        o_ref[head0 + hh, r, usl, :] = finish(
                            kind, a, lambda: (tab_ref[0, r, usl, :], tab_ref[1, r, usl, :]))


def _layer_block(shape, ngrid, col=0):
    return pl.BlockSpec((None,) + shape,
                        lambda *a: (a[ngrid][0],) + (0,) * (len(shape) - 1) + (col,),
                        pipeline_mode=pl.Buffered(1))


def _project(layer, xf, g, w, tabs, tabs4, tabs16, *, batch, seq, tm=512):
    nrow = seq // tm
    d4, d16 = B_PAIRS[1][1], B_PAIRS[2][1]
    n1 = sum(1 for _, grp in PROJ_TILES if grp == 0) * N_HEADS
    nd = len(B_KINDS) * N_HEADS
    res = lambda lead, dil: pl.BlockSpec((lead, None, dil, tm // dil, HEAD_W),
                                         lambda b, i, l: (0, b, 0, i, 0))
    shp = lambda lead, dil: jax.ShapeDtypeStruct((lead, batch, dil, seq // dil, HEAD_W), BF16)
    grid_spec = pltpu.PrefetchScalarGridSpec(
        num_scalar_prefetch=1,
        grid=(batch, nrow),
        in_specs=[
            pl.BlockSpec((tm, D_MODEL), lambda b, i, l: (b * nrow + i, 0)),
            _layer_block((1, D_MODEL), 2),
            _layer_block((D_MODEL, len(PROJ_TILES) * COL_TILE), 2),
            pl.BlockSpec((4, tm, HEAD_W), lambda b, i, l: (0, b * nrow + i, 0)),
            res(2, d4), res(2, d16),
        ],
        out_specs=[res(n1, 1), res(nd, d4), res(nd, d16)],
        scratch_shapes=[pltpu.VMEM((tm, D_MODEL), BF16),
                        pltpu.VMEM((2, N_HEADS, PROJ_ROWS, HEAD_W), F32)],
    )
    return pl.pallas_call(
        functools.partial(_proj_kernel, tm=tm),
        grid_spec=grid_spec,
        out_shape=[shp(n1, 1), shp(nd, d4), shp(nd, d16)],
        compiler_params=_cparams(("parallel", "parallel")),
        name="in_proj",
    )(layer, xf, g, w, tabs, tabs4, tabs16)


def _diff_attn_body(lam_ref, g_ref, q_ref, k_ref, v_ref, o_ref, m_scr, l_scr, acc_scr, *, tk, qs):
    tq = q_ref.shape[0]
    q = q_ref[...]
    comp = (lax.broadcasted_iota(jnp.int32, q.shape, 1) // (A_QK_DIM // 2)) % 2
    zero = jnp.zeros_like(q)
    qc = (jnp.where(comp == 0, q, zero), jnp.where(comp == 1, q, zero))
    m_scr[...] = jnp.full(m_scr.shape, -jnp.inf, F32)
    l_scr[...] = jnp.zeros(l_scr.shape, F32)
    acc_scr[...] = jnp.zeros(acc_scr.shape, F32)
    ones = jnp.ones((tk, HEAD_W), BF16)

    def block(j, d):
        base = j * tk
        for r in range(tq // qs):
            rows = slice(r * qs, (r + 1) * qs)
            off = 0 if d is None else r * qs - d * tk
            if d is not None and off + qs <= 0:
                continue
            masked = d is not None and off < tk - 1
            w = min(tk, off + qs) if masked else tk
            k = k_ref[pl.ds(base, w), :]
            vext = jnp.concatenate([v_ref[pl.ds(base, w), :], ones[:w]], axis=1)
            for c in range(2):
                s = lax.dot_general(qc[c][rows], k, (((1,), (1,)), ((), ())),
                                    preferred_element_type=F32)
                if masked:
                    row = lax.broadcasted_iota(jnp.int32, s.shape, 0) + off
                    col = lax.broadcasted_iota(jnp.int32, s.shape, 1)
                    s = jnp.where(col <= row, s, -jnp.inf)
                m_prev = m_scr[c, rows, :]
                m_new = jnp.maximum(m_prev, jnp.max(s, axis=-1, keepdims=True))
                p = jnp.exp(s - jnp.tile(m_new, (1, w // HEAD_W)))
                alpha = jnp.exp(m_prev - m_new)
                pv = jnp.dot(p.astype(BF16), vext, preferred_element_type=F32)
                l_scr[c, rows, :] = alpha * l_scr[c, rows, :] + pv[:, HEAD_W:]
                acc_scr[c, rows, :] = alpha * acc_scr[c, rows, :] + pv[:, :HEAD_W]
                m_scr[c, rows, :] = m_new

    for d in range(tq // tk):
        block(d, d)

    lv = lam_ref[...]
    lam_init = lv[4:5, 0:1]
    lam = (jnp.exp(jnp.sum(lv[0:1] * lv[1:2], axis=-1, keepdims=True))
           - jnp.exp(jnp.sum(lv[2:3] * lv[3:4], axis=-1, keepdims=True)) + lam_init)
    for r in range(tq // qs):
        rows = slice(r * qs, (r + 1) * qs)
        o = acc_scr[0, rows, :] / l_scr[0, rows, :] - lam * (acc_scr[1, rows, :] / l_scr[1, rows, :])
        o_ref[rows, :] = (_rms(o, g_ref[...]) * (1.0 - lam_init)).astype(BF16)


DIL_SPAN = B_BLOCK * B_PAIRS[-1][1]


def _band_bias(has_prev):
    shape = (B_BLOCK, 2 * B_BLOCK)
    qi = lax.broadcasted_iota(jnp.int32, shape, 0)
    kj = lax.broadcasted_iota(jnp.int32, shape, 1)
    valid = (kj >= qi) & (kj <= qi + B_BLOCK)
    if not has_prev:
        valid = valid & (kj >= B_BLOCK)
    return jnp.where(valid, 0.0, -jnp.inf).astype(F32)


def _band_unit(q, kk, vv, bias):
    s = lax.dot_general(q, kk, (((1,), (1,)), ((), ())), preferred_element_type=F32) + bias
    m = jnp.max(s, axis=-1, keepdims=True)
    scale = HEAD_W ** -0.5
    p = jnp.exp2((s - m) * (scale * math.log2(math.e)))
    ones = jnp.ones((2 * B_BLOCK, HEAD_W), BF16)
    pv = jnp.dot(p.astype(BF16), jnp.concatenate([vv, ones], axis=1), preferred_element_type=F32)
    den = pv[:, HEAD_W:]
    return pv[:, :HEAD_W] / den, m * scale + jnp.log(den)


def _dilated_body(groups, o_ref, o_scr, lse_scr):
    seq = o_ref.shape[0]
    bias = {False: _band_bias(False), True: _band_bias(True)}

    for span in range(seq // DIL_SPAN):
        for g, (q, k, v) in enumerate(groups):
            dil = B_PAIRS[g][1]
            nblk = DIL_SPAN // (dil * B_BLOCK)
            for r in range(dil):
                at = (lambda ref, sl: ref[sl, :]) if g == 0 else (lambda ref, sl, r=r: ref[r, sl, :])
                for ub in range(nblk):
                    n = span * nblk + ub
                    cur = slice(n * B_BLOCK, (n + 1) * B_BLOCK)
                    if n == 0:
                        kk = jnp.concatenate([at(k, cur)] * 2, axis=0)
                        vv = jnp.concatenate([at(v, cur)] * 2, axis=0)
                    else:
                        both = slice((n - 1) * B_BLOCK, (n + 1) * B_BLOCK)
                        kk, vv = at(k, both), at(v, both)
                    o, lse = _band_unit(at(q, cur), kk, vv, bias[n > 0])
                    rows = pl.ds(ub * B_BLOCK * dil + r, B_BLOCK, stride=dil) if dil > 1 else \
                        slice(ub * B_BLOCK, (ub + 1) * B_BLOCK)
                    o_scr[g, rows, :] = o
                    lse_scr[g, rows, :] = lse

        chunk = 256
        for c in range(DIL_SPAN // chunk):
            rows = slice(c * chunk, (c + 1) * chunk)
            ls = [lse_scr[g, rows, :] for g in range(3)]
            mx = jnp.maximum(jnp.maximum(ls[0], ls[1]), ls[2])
            es = [jnp.exp(l - mx) for l in ls]
            num = es[0] * o_scr[0, rows, :] + es[1] * o_scr[1, rows, :] + es[2] * o_scr[2, rows, :]
            out_rows = slice(span * DIL_SPAN + c * chunk, span * DIL_SPAN + (c + 1) * chunk)
            o_ref[out_rows, :] = (num / (es[0] + es[1] + es[2])).astype(BF16)


def _diff_attn_kernel(lam_ref, g_ref, qa, ka, va, oa_ref, m_scr, l_scr, acc_scr):
    _diff_attn_body(lam_ref, g_ref, qa, ka, va, oa_ref, m_scr, l_scr, acc_scr, tk=512, qs=512)


def _dilated_kernel(q1, k1, v1, q2, k2, v2, q3, k3, v3, ob_ref, o_scr, lse_scr):
    _dilated_body(((q1, k1, v1), (q2, k2, v2), (q3, k3, v3)), ob_ref, o_scr, lse_scr)


def _token_mixers(pa, pb, pc, lam_p, head_g, *, batch, seq, lam_init):
    lam_p = jnp.concatenate([lam_p, jnp.full((1, A_QK_DIM), lam_init, F32)], axis=0)

    def qkv_specs(bases, dil):
        shape = (None, None, None, seq, HEAD_W) if dil == 1 else (None, None, dil, seq // dil, HEAD_W)
        return [pl.BlockSpec(shape, lambda b, h, base=base: (base + h, b, 0, 0, 0)) for base in bases]

    out_spec = pl.BlockSpec((None, seq, HEAD_W), lambda b, h: (h, b, 0))
    out_shape = jax.ShapeDtypeStruct((N_HEADS, batch * seq, HEAD_W), BF16)
    a_bases = (0, N_HEADS, 2 * N_HEADS)
    g0_bases = (3 * N_HEADS, 4 * N_HEADS, 5 * N_HEADS)
    oa = pl.pallas_call(
        _diff_attn_kernel,
        grid=(batch, N_HEADS),
        in_specs=[pl.BlockSpec((5, A_QK_DIM), lambda b, h: (0, 0)),
                  pl.BlockSpec((1, HEAD_W), lambda b, h: (0, 0))] + qkv_specs(a_bases, 1),
        out_specs=out_spec,
        out_shape=out_shape,
        scratch_shapes=[pltpu.VMEM((2, seq, HEAD_W), F32)] * 3,
        compiler_params=_cparams(("parallel", "parallel")),
        name="diff_attention",
    )(lam_p, head_g, pa, pa, pa)
    ob = pl.pallas_call(
        _dilated_kernel,
        grid=(batch, N_HEADS),
        in_specs=(qkv_specs(g0_bases, 1) + qkv_specs(a_bases, B_PAIRS[1][1])
                  + qkv_specs(a_bases, B_PAIRS[2][1])),
        out_specs=out_spec,
        out_shape=out_shape,
        scratch_shapes=[pltpu.VMEM((3, DIL_SPAN, HEAD_W), F32)] * 2,
        compiler_params=_cparams(("parallel", "parallel")),
        name="dilated_attention",
    )(pa, pa, pa, pb, pb, pb, pc, pc, pc)
    return oa, ob


def _merge_kernel(layer_ref, x_ref, oa_ref, ob_ref, ga_ref, gb_ref, wa_ref, wb_ref, wm_ref, g_ref,
                  o_ref):
    heads = lambda ref, n: jnp.concatenate([ref[c] for c in range(n)], axis=-1)
    ya = jnp.dot(heads(oa_ref, N_HEADS), wa_ref[...], preferred_element_type=F32)
    yb = jnp.dot(heads(ob_ref, N_HEADS), wb_ref[...], preferred_element_type=F32)
    ntile = D_MODEL // HEAD_W
    z = heads(ga_ref, ntile).astype(F32) * ya + heads(gb_ref, ntile).astype(F32) * yb
    mix = jnp.dot(z.astype(BF16), wm_ref[...], preferred_element_type=F32)
    o_ref[...] = x_ref[...] + _rms(mix, g_ref[...])


def _merge(layer, xf, oa, ob, pa, wa, wb, wm, g, *, batch, seq, tm=512):
    t = batch * seq
    nrow = seq // tm
    ntile = D_MODEL // HEAD_W
    gate_base = 6 * N_HEADS // ntile
    gate = lambda off: pl.BlockSpec((ntile, None, None, tm, HEAD_W),
                                    lambda i, l: (gate_base + off, i // nrow, 0, i % nrow, 0))
    grid_spec = pltpu.PrefetchScalarGridSpec(
        num_scalar_prefetch=1,
        grid=(t // tm,),
        in_specs=[
            pl.BlockSpec((tm, D_MODEL), lambda i, l: (i, 0)),
            pl.BlockSpec((N_HEADS, tm, HEAD_W), lambda i, l: (0, i, 0)),
            pl.BlockSpec((N_HEADS, tm, HEAD_W), lambda i, l: (0, i, 0)),
            gate(0), gate(1),
            _layer_block((N_HEADS * HEAD_W, D_MODEL), 1), _layer_block((N_HEADS * HEAD_W, D_MODEL), 1),
            _layer_block((D_MODEL, D_MODEL), 1), _layer_block((1, D_MODEL), 1),
        ],
        out_specs=pl.BlockSpec((tm, D_MODEL), lambda i, l: (i, 0)),
    )
    return pl.pallas_call(
        _merge_kernel,
        grid_spec=grid_spec,
        out_shape=jax.ShapeDtypeStruct((t, D_MODEL), F32),
        compiler_params=_cparams(("parallel",)),
        name="merge",
    )(layer, xf, oa, ob, pa, pa, wa, wb, wm, g)


FFN_HALO = 16


def _ffn_kernel(layer_ref, x_ref, xh_ref, gpre_ref, wg_ref, wv_ref, cw_ref, cb_ref, wd_ref, gpost_ref,
                o_ref, h_scr, *, tiles_per_seq):
    i = pl.program_id(0)
    hh = _rms(xh_ref[...], gpre_ref[...])
    hh = jnp.where(i % tiles_per_seq == 0, jnp.zeros_like(hh), hh)
    h_scr[0:FFN_HALO, :] = hh.astype(BF16)
    h_scr[FFN_HALO:, :] = _rms(x_ref[...], gpre_ref[...]).astype(BF16)

    def conv(u, cw, cb):
        u1 = pltpu.roll(u, 1, 0)[FFN_HALO:, :]
        u2 = pltpu.roll(u, 2, 0)[FFN_HALO:, :]
        return cw[2:3, :] * u[FFN_HALO:, :] + cw[1:2, :] * u1 + cw[0:1, :] * u2 + cb

    h = h_scr[...]
    gate = conv(jnp.dot(h, wg_ref[...], preferred_element_type=F32),
                cw_ref[:, :D_FF], cb_ref[:, :D_FF])
    half_val = conv(jnp.dot(h, wv_ref[...], preferred_element_type=F32),
                    0.5 * cw_ref[:, D_FF:], 0.5 * cb_ref[:, D_FF:])
    c = math.sqrt(2.0 / math.pi)
    inner = gate * (c + (c * 0.044715) * (gate * gate))
    act = ((gate * half_val) * (1.0 + jnp.tanh(inner))).astype(BF16)
    y = jnp.dot(act, wd_ref[...], preferred_element_type=F32)
    o_ref[...] = x_ref[...] + _rms(y, gpost_ref[...])


def _ffn(layer, xf, gpre, w_up, conv_w, conv_b, w_down, gpost, *, seq, tm=512):
    t = xf.shape[0]
    tiles_per_seq = seq // tm
    hb = tm // FFN_HALO
    kern = functools.partial(_ffn_kernel, tiles_per_seq=tiles_per_seq)
    const = lambda shape, col=0: _layer_block(shape, 1, col)
    grid_spec = pltpu.PrefetchScalarGridSpec(
        num_scalar_prefetch=1,
        grid=(t // tm,),
        in_specs=[
            pl.BlockSpec((tm, D_MODEL), lambda i, l: (i, 0)),
            pl.BlockSpec((FFN_HALO, D_MODEL), lambda i, l: (jnp.maximum(i * hb - 1, 0), 0)),
            const((1, D_MODEL)),
            const((D_MODEL, D_FF), 0), const((D_MODEL, D_FF), 1),
            const((3, 2 * D_FF)), const((1, 2 * D_FF)),
            const((D_FF, D_MODEL)), const((1, D_MODEL)),
        ],
        out_specs=pl.BlockSpec((tm, D_MODEL), lambda i, l: (i, 0)),
        scratch_shapes=[pltpu.VMEM((tm + FFN_HALO, D_MODEL), BF16)],
    )
    return pl.pallas_call(
        kern,
        grid_spec=grid_spec,
        out_shape=jax.ShapeDtypeStruct((t, D_MODEL), F32),
        compiler_params=_cparams(("parallel",)),
        name="conv_glu",
    )(layer, xf, xf, gpre, w_up, w_up, conv_w, conv_b, w_down, gpost)


def kernel(x, positions, pre_mix_g, w_in, diff_lambda, diff_head_g, w_a_out, w_b_out, w_mix_out,
           post_mix_g, pre_ffn_g, w_up, conv_w, conv_b, w_down, post_ffn_g):
    batch, seq, d = x.shape
    assert d == D_MODEL and seq % DIL_SPAN == 0
    depth = w_in.shape[0]
    t = batch * seq
    xf = x.reshape(t, d)
    tabs = _rope_tables(positions.reshape(t, 1))
    d4, d16 = B_PAIRS[1][1], B_PAIRS[2][1]
    res_major = lambda dil: tabs[2:].reshape(2, batch, seq // dil, dil, HEAD_W).swapaxes(2, 3)
    tabs_d4, tabs_d16 = res_major(d4), res_major(d16)

    a_w = N_HEADS * HEAD_W
    b0 = 3 * a_w
    gsl = lambda part, g: slice(b0 + (3 * part + g) * a_w, b0 + (3 * part + g + 1) * a_w)
    row = lambda v: v.reshape(1, -1)
    half = A_QK_DIM // 2

    def regroup_a(w):
        return w.reshape(depth, d, N_HEADS, 2, 2, half).swapaxes(3, 4).reshape(depth, d, a_w)

    w_all = jnp.concatenate(
        [regroup_a(w_in[:, :, :a_w]) * (A_QK_DIM ** -0.5), regroup_a(w_in[:, :, a_w:2 * a_w]),
         w_in[:, :, 2 * a_w:b0], w_in[:, :, gsl(0, 0)], w_in[:, :, gsl(1, 0)], w_in[:, :, gsl(2, 0)],
         w_in[:, :, b0 + 9 * a_w:]]
        + [w_in[:, :, gsl(p, 1)] for p in range(3)] + [w_in[:, :, gsl(p, 2)] for p in range(3)],
        axis=2).astype(BF16)
    rows = lambda v: v.reshape(depth, 1, -1)
    wa, wb, wm = w_a_out.astype(BF16), w_b_out.astype(BF16), w_mix_out.astype(BF16)
    wu, wd = w_up.astype(BF16), w_down.astype(BF16)
    g_pre, g_post, g_pre_ffn, g_post_ffn = (rows(pre_mix_g), rows(post_mix_g), rows(pre_ffn_g),
                                            rows(post_ffn_g))
    cb = rows(conv_b)

    for l in range(depth):
        layer = jnp.full((1,), l, jnp.int32)
        pa, pb, pc = _project(layer, xf, g_pre, w_all, tabs, tabs_d4, tabs_d16, batch=batch, seq=seq)
        lam_init = 0.8 - 0.6 * math.exp(-0.3 * l)
        oa, ob = _token_mixers(pa, pb, pc, diff_lambda[l], row(diff_head_g[l]), batch=batch, seq=seq,
                               lam_init=lam_init)
        xf = _merge(layer, xf, oa, ob, pa, wa, wb, wm, g_post, batch=batch, seq=seq)
        xf = _ffn(layer, xf, g_pre_ffn, wu, conv_w, cb, wd, g_post_ffn, seq=seq)
    return xf.reshape(batch, seq, d)
```

```python
import functools
import math

import jax
import jax.numpy as jnp
import numpy as np
from jax import lax
from jax.experimental import pallas as pl
from jax.experimental.pallas import tpu as pltpu

F32 = jnp.float32
BF16 = jnp.bfloat16

D_MODEL = 1024
N_HEADS = 4
HEAD_W = 128
A_QK_DIM = 64
B_PAIRS = ((128, 1), (512, 4), (2048, 16))
B_BLOCK = 128
D_FF = 2816
ROPE_THETA = 10000.0
NORM_EPS = 1e-6
COL_TILE = N_HEADS * HEAD_W
VMEM_LIMIT = 56 * 1024 * 1024

K_ROPE_A, K_ROPE_B, K_PLAIN, K_GATE = range(4)


def _cparams(sem):
    return pltpu.CompilerParams(dimension_semantics=sem, vmem_limit_bytes=VMEM_LIMIT)


def _rms(x, g):
    return x * lax.rsqrt(jnp.mean(x * x, axis=-1, keepdims=True) + NORM_EPS) * g


def _rope_table_kernel(pos_ref, f_ref, o_ref):
    pos = pos_ref[...].astype(F32)
    ang = pos * f_ref[0:1, :]
    lane = lax.broadcasted_iota(jnp.int32, ang.shape, 1)
    sign = f_ref[1:2, :]
    half, quarter = HEAD_W // 2, HEAD_W // 4

    def spread(t):
        up = pltpu.roll(t, half, 1)
        tab_b = jnp.where(lane < half, t, up)
        blk = lane // quarter
        tab_a = jnp.where(blk == 0, up,
                          jnp.where(blk == 1, pltpu.roll(t, half + quarter, 1),
                                    jnp.where(blk == 2, t, pltpu.roll(t, quarter, 1))))
        return tab_a, tab_b

    cos_a, cos_b = spread(jnp.cos(ang))
    sin_a, sin_b = spread(jnp.sin(ang))
    o_ref[0] = cos_a
    o_ref[1] = sin_a * sign
    o_ref[2] = cos_b
    o_ref[3] = sin_b * sign


def _rope_tables(pos_col):
    t = pos_col.shape[0]
    tm = 1024
    lane = np.arange(HEAD_W)
    sign = jnp.asarray(np.where(lane < HEAD_W // 2, -1.0, 1.0), F32)
    inv_a = ROPE_THETA ** (-jnp.arange(0, A_QK_DIM, 2, dtype=F32) / A_QK_DIM)
    inv_b = ROPE_THETA ** (-jnp.arange(0, HEAD_W, 2, dtype=F32) / HEAD_W)
    packed = jnp.concatenate([inv_b, inv_a, jnp.zeros((HEAD_W // 4,), F32)])
    rows = [packed, sign]
    ftab = jnp.stack(rows + [jnp.zeros((HEAD_W,), F32)] * (8 - len(rows)), axis=0)
    return pl.pallas_call(
        _rope_table_kernel,
        grid=(t // tm,),
        in_specs=[pl.BlockSpec((tm, 1), lambda i: (i, 0)),
                  pl.BlockSpec((8, HEAD_W), lambda i: (0, 0))],
        out_specs=pl.BlockSpec((4, tm, HEAD_W), lambda i: (0, i, 0)),
        out_shape=jax.ShapeDtypeStruct((4, t, HEAD_W), F32),
        compiler_params=_cparams(("parallel",)),
        name="rope_tables",
    )(pos_col, ftab)


PROJ_ROWS = 256
B_KINDS = (K_ROPE_B, K_ROPE_B, K_PLAIN)
PROJ_TILES = (tuple((k, 0) for k in (K_ROPE_A, K_ROPE_A, K_PLAIN) + B_KINDS + (K_GATE,) * 4)
              + tuple((k, 1) for k in B_KINDS) + tuple((k, 2) for k in B_KINDS))


def _proj_kernel(layer_ref, x_ref, g_ref, w_ref, tab1_ref, tab4_ref, tab16_ref,
                 o1_ref, o4_ref, o16_ref, h_scr, acc_scr, *, tm):
    rc = PROJ_ROWS
    h_scr[...] = _rms(x_ref[...], g_ref[...]).astype(BF16)
    tabs = (tab1_ref, tab4_ref, tab16_ref)
    outs = (o1_ref, o4_ref, o16_ref)

    def finish(kind, a, cos_sin):
        if kind in (K_ROPE_A, K_ROPE_B):
            cos, sin = cos_sin()
            out = a * cos + pltpu.roll(a, HEAD_W // 2, 1) * sin
        elif kind == K_GATE:
            out = jax.nn.sigmoid(a)
        else:
            out = a
        return out.astype(BF16)

    first = {}
    for ct, (kind, grp) in enumerate(PROJ_TILES):
        first.setdefault(grp, ct)
        head0 = (ct - first[grp]) * N_HEADS
        dil = B_PAIRS[grp][1]
        n = rc // dil
        tab_ref, o_ref = tabs[grp], outs[grp]
        t0 = 2 if (kind == K_ROPE_B and grp == 0) else 0
        for c in range(tm // rc):
            rsl = slice(c * rc, (c + 1) * rc)
            acc = jnp.dot(h_scr[rsl, :], w_ref[:, ct * COL_TILE:(ct + 1) * COL_TILE],
                          preferred_element_type=F32)
            slot = (ct + c) % 2
            for hh in range(N_HEADS):
                a = acc[:, hh * HEAD_W:(hh + 1) * HEAD_W]
                if dil == 1:
                    o_ref[head0 + hh, 0, rsl, :] = finish(
                        kind, a, lambda: (tab_ref[t0, rsl, :], tab_ref[t0 + 1, rsl, :]))
                else:
                    acc_scr[slot, hh] = a
            if dil > 1:
                usl = slice(c * n, (c + 1) * n)
                for hh in range(N_HEADS):
                    for r in range(dil):
                        a = acc_scr[slot, hh, pl.ds(r, n, stride=dil), :]
                        o_ref[head0 + hh, r, usl, :] = finish(
                            kind, a, lambda: (tab_ref[0, r, usl, :], tab_ref[1, r, usl, :]))


def _layer_block(shape, ngrid, col=0):
    return pl.BlockSpec((None,) + shape,
                        lambda *a: (a[ngrid][0],) + (0,) * (len(shape) - 1) + (col,),
                        pipeline_mode=pl.Buffered(1))


def _project(layer, xf, g, w, tabs, tabs4, tabs16, *, batch, seq, tm=512):
    nrow = seq // tm
    d4, d16 = B_PAIRS[1][1], B_PAIRS[2][1]
    n1 = sum(1 for _, grp in PROJ_TILES if grp == 0) * N_HEADS
    nd = len(B_KINDS) * N_HEADS
    res = lambda lead, dil: pl.BlockSpec((lead, None, dil, tm // dil, HEAD_W),
                                         lambda b, i, l: (0, b, 0, i, 0))
    shp = lambda lead, dil: jax.ShapeDtypeStruct((lead, batch, dil, seq // dil, HEAD_W), BF16)
    grid_spec = pltpu.PrefetchScalarGridSpec(
        num_scalar_prefetch=1,
        grid=(batch, nrow),
        in_specs=[
            pl.BlockSpec((tm, D_MODEL), lambda b, i, l: (b * nrow + i, 0)),
            _layer_block((1, D_MODEL), 2),
            _layer_block((D_MODEL, len(PROJ_TILES) * COL_TILE), 2),
            pl.BlockSpec((4, tm, HEAD_W), lambda b, i, l: (0, b * nrow + i, 0)),
            res(2, d4), res(2, d16),
        ],
        out_specs=[res(n1, 1), res(nd, d4), res(nd, d16)],
        scratch_shapes=[pltpu.VMEM((tm, D_MODEL), BF16),
                        pltpu.VMEM((2, N_HEADS, PROJ_ROWS, HEAD_W), F32)],
    )
    return pl.pallas_call(
        functools.partial(_proj_kernel, tm=tm),
        grid_spec=grid_spec,
        out_shape=[shp(n1, 1), shp(nd, d4), shp(nd, d16)],
        compiler_params=_cparams(("parallel", "parallel")),
        name="in_proj",
    )(layer, xf, g, w, tabs, tabs4, tabs16)


def _diff_attn_body(lam_ref, g_ref, q_ref, k_ref, v_ref, o_ref, m_scr, l_scr, acc_scr, *, tk, qs):
    tq = q_ref.shape[0]
    q = q_ref[...]
    comp = (lax.broadcasted_iota(jnp.int32, q.shape, 1) // (A_QK_DIM // 2)) % 2
    zero = jnp.zeros_like(q)
    qc = (jnp.where(comp == 0, q, zero), jnp.where(comp == 1, q, zero))
    m_scr[...] = jnp.full(m_scr.shape, -jnp.inf, F32)
    l_scr[...] = jnp.zeros(l_scr.shape, F32)
    acc_scr[...] = jnp.zeros(acc_scr.shape, F32)
    ones = jnp.ones((tk, HEAD_W), BF16)

    def block(j, d):
        base = j * tk
        for r in range(tq // qs):
            rows = slice(r * qs, (r + 1) * qs)
            off = 0 if d is None else r * qs - d * tk
            if d is not None and off + qs <= 0:
                continue
            masked = d is not None and off < tk - 1
            w = min(tk, off + qs) if masked else tk
            k = k_ref[pl.ds(base, w), :]
            vext = jnp.concatenate([v_ref[pl.ds(base, w), :], ones[:w]], axis=1)
            for c in range(2):
                s = lax.dot_general(qc[c][rows], k, (((1,), (1,)), ((), ())),
                                    preferred_element_type=F32)
                if masked:
                    row = lax.broadcasted_iota(jnp.int32, s.shape, 0) + off
                    col = lax.broadcasted_iota(jnp.int32, s.shape, 1)
                    s = jnp.where(col <= row, s, -jnp.inf)
                m_prev = m_scr[c, rows, :]
                m_new = jnp.maximum(m_prev, jnp.max(s, axis=-1, keepdims=True))
                p = jnp.exp(s - jnp.tile(m_new, (1, w // HEAD_W)))
                alpha = jnp.exp(m_prev - m_new)
                pv = jnp.dot(p.astype(BF16), vext, preferred_element_type=F32)
                l_scr[c, rows, :] = alpha * l_scr[c, rows, :] + pv[:, HEAD_W:]
                acc_scr[c, rows, :] = alpha * acc_scr[c, rows, :] + pv[:, :HEAD_W]
                m_scr[c, rows, :] = m_new

    for d in range(tq // tk):
        block(d, d)

    lv = lam_ref[...]
    lam_init = lv[4:5, 0:1]
    lam = (jnp.exp(jnp.sum(lv[0:1] * lv[1:2], axis=-1, keepdims=True))
           - jnp.exp(jnp.sum(lv[2:3] * lv[3:4], axis=-1, keepdims=True)) + lam_init)
    for r in range(tq // qs):
        rows = slice(r * qs, (r + 1) * qs)
        o = acc_scr[0, rows, :] / l_scr[0, rows, :] - lam * (acc_scr[1, rows, :] / l_scr[1, rows, :])
        o_ref[rows, :] = (_rms(o, g_ref[...]) * (1.0 - lam_init)).astype(BF16)


DIL_SPAN = B_BLOCK * B_PAIRS[-1][1]


def _band_bias(has_prev):
    shape = (B_BLOCK, 2 * B_BLOCK)
    qi = lax.broadcasted_iota(jnp.int32, shape, 0)
    kj = lax.broadcasted_iota(jnp.int32, shape, 1)
    valid = (kj >= qi) & (kj <= qi + B_BLOCK)
    if not has_prev:
        valid = valid & (kj >= B_BLOCK)
    return jnp.where(valid, 0.0, -jnp.inf).astype(F32)


def _band_unit(q, kk, vv, bias):
    s = lax.dot_general(q, kk, (((1,), (1,)), ((), ())), preferred_element_type=F32) + bias
    m = jnp.max(s, axis=-1, keepdims=True)
    scale = HEAD_W ** -0.5
    p = jnp.exp2((s - m) * (scale * math.log2(math.e)))
    ones = jnp.ones((2 * B_BLOCK, HEAD_W), BF16)
    pv = jnp.dot(p.astype(BF16), jnp.concatenate([vv, ones], axis=1), preferred_element_type=F32)
    den = pv[:, HEAD_W:]
    return pv[:, :HEAD_W] / den, m * scale + jnp.log(den)


def _dilated_body(groups, o_ref, o_scr, lse_scr):
    seq = o_ref.shape[0]
    bias = {False: _band_bias(False), True: _band_bias(True)}

    for span in range(seq // DIL_SPAN):
        for g, (q, k, v) in enumerate(groups):
            dil = B_PAIRS[g][1]
            nblk = DIL_SPAN // (dil * B_BLOCK)
            for r in range(dil):
                at = (lambda ref, sl: ref[sl, :]) if g == 0 else (lambda ref, sl, r=r: ref[r, sl, :])
                for ub in range(nblk):
                    n = span * nblk + ub
                    cur = slice(n * B_BLOCK, (n + 1) * B_BLOCK)
                    if n == 0:
                        kk = jnp.concatenate([at(k, cur)] * 2, axis=0)
                        vv = jnp.concatenate([at(v, cur)] * 2, axis=0)
                    else:
                        both = slice((n - 1) * B_BLOCK, (n + 1) * B_BLOCK)
                        kk, vv = at(k, both), at(v, both)
                    o, lse = _band_unit(at(q, cur), kk, vv, bias[n > 0])
                    rows = pl.ds(ub * B_BLOCK * dil + r, B_BLOCK, stride=dil) if dil > 1 else \
                        slice(ub * B_BLOCK, (ub + 1) * B_BLOCK)
                    o_scr[g, rows, :] = o
                    lse_scr[g, rows, :] = lse

        chunk = 256
        for c in range(DIL_SPAN // chunk):
            rows = slice(c * chunk, (c + 1) * chunk)
            ls = [lse_scr[g, rows, :] for g in range(3)]
            mx = jnp.maximum(jnp.maximum(ls[0], ls[1]), ls[2])
            es = [jnp.exp(l - mx) for l in ls]
            num = es[0] * o_scr[0, rows, :] + es[1] * o_scr[1, rows, :] + es[2] * o_scr[2, rows, :]
            out_rows = slice(span * DIL_SPAN + c * chunk, span * DIL_SPAN + (c + 1) * chunk)
            o_ref[out_rows, :] = (num / (es[0] + es[1] + es[2])).astype(BF16)


def _diff_attn_kernel(lam_ref, g_ref, qa, ka, va, oa_ref, m_scr, l_scr, acc_scr):
    _diff_attn_body(lam_ref, g_ref, qa, ka, va, oa_ref, m_scr, l_scr, acc_scr, tk=512, qs=512)


def _dilated_kernel(q1, k1, v1, q2, k2, v2, q3, k3, v3, ob_ref, o_scr, lse_scr):
    _dilated_body(((q1, k1, v1), (q2, k2, v2), (q3, k3, v3)), ob_ref, o_scr, lse_scr)


def _token_mixers(pa, pb, pc, lam_p, head_g, *, batch, seq, lam_init):
    lam_p = jnp.concatenate([lam_p, jnp.full((1, A_QK_DIM), lam_init, F32)], axis=0)

    def qkv_specs(bases, dil):
        shape = (None, None, None, seq, HEAD_W) if dil == 1 else (None, None, dil, seq // dil, HEAD_W)
        return [pl.BlockSpec(shape, lambda b, h, base=base: (base + h, b, 0, 0, 0)) for base in bases]

    out_spec = pl.BlockSpec((None, seq, HEAD_W), lambda b, h: (h, b, 0))
    out_shape = jax.ShapeDtypeStruct((N_HEADS, batch * seq, HEAD_W), BF16)
    a_bases = (0, N_HEADS, 2 * N_HEADS)
    g0_bases = (3 * N_HEADS, 4 * N_HEADS, 5 * N_HEADS)
    oa = pl.pallas_call(
        _diff_attn_kernel,
        grid=(batch, N_HEADS),
        in_specs=[pl.BlockSpec((5, A_QK_DIM), lambda b, h: (0, 0)),
                  pl.BlockSpec((1, HEAD_W), lambda b, h: (0, 0))] + qkv_specs(a_bases, 1),
        out_specs=out_spec,
        out_shape=out_shape,
        scratch_shapes=[pltpu.VMEM((2, seq, HEAD_W), F32)] * 3,
        compiler_params=_cparams(("parallel", "parallel")),
        name="diff_attention",
    )(lam_p, head_g, pa, pa, pa)
    ob = pl.pallas_call(
        _dilated_kernel,
        grid=(batch, N_HEADS),
        in_specs=(qkv_specs(g0_bases, 1) + qkv_specs(a_bases, B_PAIRS[1][1])
                  + qkv_specs(a_bases, B_PAIRS[2][1])),
        out_specs=out_spec,
        out_shape=out_shape,
        scratch_shapes=[pltpu.VMEM((3, DIL_SPAN, HEAD_W), F32)] * 2,
        compiler_params=_cparams(("parallel", "parallel")),
        name="dilated_attention",
    )(pa, pa, pa, pb, pb, pb, pc, pc, pc)
    return oa, ob


MERGE_ROWS = 512


def _merge_kernel(layer_ref, x_ref, oa_ref, ob_ref, ga_ref, gb_ref, wa_ref, wb_ref, wm_ref, g_ref,
                  o_ref):
    ntile = D_MODEL // HEAD_W
    for c in range(x_ref.shape[0] // MERGE_ROWS):
        rs = slice(c * MERGE_ROWS, (c + 1) * MERGE_ROWS)
        heads = lambda ref, n: jnp.concatenate([ref[k, rs, :] for k in range(n)], axis=-1)
        ya = jnp.dot(heads(oa_ref, N_HEADS), wa_ref[...], preferred_element_type=F32)
        yb = jnp.dot(heads(ob_ref, N_HEADS), wb_ref[...], preferred_element_type=F32)
        z = heads(ga_ref, ntile).astype(F32) * ya + heads(gb_ref, ntile).astype(F32) * yb
        mix = jnp.dot(z.astype(BF16), wm_ref[...], preferred_element_type=F32)
        o_ref[rs, :] = x_ref[rs, :] + _rms(mix, g_ref[...])


def _merge(layer, xf, oa, ob, pa, wa, wb, wm, g, *, batch, seq, tm=2 * MERGE_ROWS):
    t = batch * seq
    nrow = seq // tm
    ntile = D_MODEL // HEAD_W
    gate_base = 6 * N_HEADS // ntile
    gate = lambda off: pl.BlockSpec((ntile, None, None, tm, HEAD_W),
                                    lambda i, l: (gate_base + off, i // nrow, 0, i % nrow, 0))
    grid_spec = pltpu.PrefetchScalarGridSpec(
        num_scalar_prefetch=1,
        grid=(t // tm,),
        in_specs=[
            pl.BlockSpec((tm, D_MODEL), lambda i, l: (i, 0)),
            pl.BlockSpec((N_HEADS, tm, HEAD_W), lambda i, l: (0, i, 0)),
            pl.BlockSpec((N_HEADS, tm, HEAD_W), lambda i, l: (0, i, 0)),
            gate(0), gate(1),
            _layer_block((N_HEADS * HEAD_W, D_MODEL), 1), _layer_block((N_HEADS * HEAD_W, D_MODEL), 1),
            _layer_block((D_MODEL, D_MODEL), 1), _layer_block((1, D_MODEL), 1),
        ],
        out_specs=pl.BlockSpec((tm, D_MODEL), lambda i, l: (i, 0)),
    )
    return pl.pallas_call(
        _merge_kernel,
        grid_spec=grid_spec,
        out_shape=jax.ShapeDtypeStruct((t, D_MODEL), F32),
        compiler_params=_cparams(("parallel",)),
        name="merge",
    )(layer, xf, oa, ob, pa, pa, wa, wb, wm, g)


FFN_HALO = 16


def _ffn_kernel(layer_ref, x_ref, xh_ref, gpre_ref, wg_ref, wv_ref, cw_ref, cb_ref, wd_ref, gpost_ref,
                o_ref, h_scr, *, tiles_per_seq):
    i = pl.program_id(0)
    hh = _rms(xh_ref[...], gpre_ref[...])
    hh = jnp.where(i % tiles_per_seq == 0, jnp.zeros_like(hh), hh)
    h_scr[0:FFN_HALO, :] = hh.astype(BF16)
    h_scr[FFN_HALO:, :] = _rms(x_ref[...], gpre_ref[...]).astype(BF16)

    def conv(u, cw, cb):
        u1 = pltpu.roll(u, 1, 0)[FFN_HALO:, :]
        u2 = pltpu.roll(u, 2, 0)[FFN_HALO:, :]
        return cw[2:3, :] * u[FFN_HALO:, :] + cw[1:2, :] * u1 + cw[0:1, :] * u2 + cb

    h = h_scr[...]
    gate = conv(jnp.dot(h, wg_ref[...], preferred_element_type=F32),
                cw_ref[:, :D_FF], cb_ref[:, :D_FF])
    half_val = conv(jnp.dot(h, wv_ref[...], preferred_element_type=F32),
                    0.5 * cw_ref[:, D_FF:], 0.5 * cb_ref[:, D_FF:])
    c = math.sqrt(2.0 / math.pi)
    inner = gate * (c + (c * 0.044715) * (gate * gate))
    act = ((gate * half_val) * (1.0 + jnp.tanh(inner))).astype(BF16)
    y = jnp.dot(act, wd_ref[...], preferred_element_type=F32)
    o_ref[...] = x_ref[...] + _rms(y, gpost_ref[...])


def _ffn(layer, xf, gpre, w_up, conv_w, conv_b, w_down, gpost, *, seq, tm=512):
    t = xf.shape[0]
    tiles_per_seq = seq // tm
    hb = tm // FFN_HALO
    kern = functools.partial(_ffn_kernel, tiles_per_seq=tiles_per_seq)
    const = lambda shape, col=0: _layer_block(shape, 1, col)
    grid_spec = pltpu.PrefetchScalarGridSpec(
        num_scalar_prefetch=1,
        grid=(t // tm,),
        in_specs=[
            pl.BlockSpec((tm, D_MODEL), lambda i, l: (i, 0)),
            pl.BlockSpec((FFN_HALO, D_MODEL), lambda i, l: (jnp.maximum(i * hb - 1, 0), 0)),
            const((1, D_MODEL)),
            const((D_MODEL, D_FF), 0), const((D_MODEL, D_FF), 1),
            const((3, 2 * D_FF)), const((1, 2 * D_FF)),
            const((D_FF, D_MODEL)), const((1, D_MODEL)),
        ],
        out_specs=pl.BlockSpec((tm, D_MODEL), lambda i, l: (i, 0)),
        scratch_shapes=[pltpu.VMEM((tm + FFN_HALO, D_MODEL), BF16)],
    )
    return pl.pallas_call(
        kern,
        grid_spec=grid_spec,
        out_shape=jax.ShapeDtypeStruct((t, D_MODEL), F32),
        compiler_params=_cparams(("parallel",)),
        name="conv_glu",
    )(layer, xf, xf, gpre, w_up, w_up, conv_w, conv_b, w_down, gpost)


def kernel(x, positions, pre_mix_g, w_in, diff_lambda, diff_head_g, w_a_out, w_b_out, w_mix_out,
           post_mix_g, pre_ffn_g, w_up, conv_w, conv_b, w_down, post_ffn_g):
    batch, seq, d = x.shape
    assert d == D_MODEL and seq % DIL_SPAN == 0
    depth = w_in.shape[0]
    t = batch * seq
    xf = x.reshape(t, d)
    tabs = _rope_tables(positions.reshape(t, 1))
    d4, d16 = B_PAIRS[1][1], B_PAIRS[2][1]
    res_major = lambda dil: tabs[2:].reshape(2, batch, seq // dil, dil, HEAD_W).swapaxes(2, 3)
    tabs_d4, tabs_d16 = res_major(d4), res_major(d16)

    a_w = N_HEADS * HEAD_W
    b0 = 3 * a_w
    gsl = lambda part, g: slice(b0 + (3 * part + g) * a_w, b0 + (3 * part + g + 1) * a_w)
    row = lambda v: v.reshape(1, -1)
    half = A_QK_DIM // 2

    def regroup_a(w):
        return w.reshape(depth, d, N_HEADS, 2, 2, half).swapaxes(3, 4).reshape(depth, d, a_w)

    w_all = jnp.concatenate(
        [regroup_a(w_in[:, :, :a_w]) * (A_QK_DIM ** -0.5), regroup_a(w_in[:, :, a_w:2 * a_w]),
         w_in[:, :, 2 * a_w:b0], w_in[:, :, gsl(0, 0)], w_in[:, :, gsl(1, 0)], w_in[:, :, gsl(2, 0)],
         w_in[:, :, b0 + 9 * a_w:]]
        + [w_in[:, :, gsl(p, 1)] for p in range(3)] + [w_in[:, :, gsl(p, 2)] for p in range(3)],
        axis=2).astype(BF16)
    rows = lambda v: v.reshape(depth, 1, -1)
    wa, wb, wm = w_a_out.astype(BF16), w_b_out.astype(BF16), w_mix_out.astype(BF16)
    wu, wd = w_up.astype(BF16), w_down.astype(BF16)
    g_pre, g_post, g_pre_ffn, g_post_ffn = (rows(pre_mix_g), rows(post_mix_g), rows(pre_ffn_g),
                                            rows(post_ffn_g))
    cb = rows(conv_b)

    for l in range(depth):
        layer = jnp.full((1,), l, jnp.int32)
        pa, pb, pc = _project(layer, xf, g_pre, w_all, tabs, tabs_d4, tabs_d16, batch=batch, seq=seq)
        lam_init = 0.8 - 0.6 * math.exp(-0.3 * l)
        oa, ob = _token_mixers(pa, pb, pc, diff_lambda[l], row(diff_head_g[l]), batch=batch, seq=seq,
                               lam_init=lam_init)
        xf = _merge(layer, xf, oa, ob, pa, wa, wb, wm, g_post, batch=batch, seq=seq)
        xf = _ffn(layer, xf, g_pre_ffn, wu, conv_w, cb, wd, g_post_ffn, seq=seq)
    return xf.reshape(batch, seq, d)
```

```python
import functools
import math

import jax
import jax.numpy as jnp
import numpy as np
from jax import lax
from jax.experimental import pallas as pl
from jax.experimental.pallas import tpu as pltpu

F32 = jnp.float32
BF16 = jnp.bfloat16

D_MODEL = 1024
N_HEADS = 4
HEAD_W = 128
A_QK_DIM = 64
B_PAIRS = ((128, 1), (512, 4), (2048, 16))
B_BLOCK = 128
D_FF = 2816
ROPE_THETA = 10000.0
NORM_EPS = 1e-6
COL_TILE = N_HEADS * HEAD_W
VMEM_LIMIT = 56 * 1024 * 1024

K_ROPE_A, K_ROPE_B, K_PLAIN, K_GATE = range(4)


def _cparams(sem):
    return pltpu.CompilerParams(dimension_semantics=sem, vmem_limit_bytes=VMEM_LIMIT)


def _rms(x, g):
    return x * lax.rsqrt(jnp.mean(x * x, axis=-1, keepdims=True) + NORM_EPS) * g


def _rope_table_kernel(pos_ref, f_ref, o_ref):
    pos = pos_ref[...].astype(F32)
    ang = pos * f_ref[0:1, :]
    lane = lax.broadcasted_iota(jnp.int32, ang.shape, 1)
    sign = f_ref[1:2, :]
    half, quarter = HEAD_W // 2, HEAD_W // 4

    def spread(t):
        up = pltpu.roll(t, half, 1)
        tab_b = jnp.where(lane < half, t, up)
        blk = lane // quarter
        tab_a = jnp.where(blk == 0, up,
                          jnp.where(blk == 1, pltpu.roll(t, half + quarter, 1),
                                    jnp.where(blk == 2, t, pltpu.roll(t, quarter, 1))))
        return tab_a, tab_b

    cos_a, cos_b = spread(jnp.cos(ang))
    sin_a, sin_b = spread(jnp.sin(ang))
    o_ref[0] = cos_a
    o_ref[1] = sin_a * sign
    o_ref[2] = cos_b
    o_ref[3] = sin_b * sign


def _rope_tables(pos_col):
    t = pos_col.shape[0]
    tm = 1024
    lane = np.arange(HEAD_W)
    sign = jnp.asarray(np.where(lane < HEAD_W // 2, -1.0, 1.0), F32)
    inv_a = ROPE_THETA ** (-jnp.arange(0, A_QK_DIM, 2, dtype=F32) / A_QK_DIM)
    inv_b = ROPE_THETA ** (-jnp.arange(0, HEAD_W, 2, dtype=F32) / HEAD_W)
    packed = jnp.concatenate([inv_b, inv_a, jnp.zeros((HEAD_W // 4,), F32)])
    rows = [packed, sign]
    ftab = jnp.stack(rows + [jnp.zeros((HEAD_W,), F32)] * (8 - len(rows)), axis=0)
    return pl.pallas_call(
        _rope_table_kernel,
        grid=(t // tm,),
        in_specs=[pl.BlockSpec((tm, 1), lambda i: (i, 0)),
                  pl.BlockSpec((8, HEAD_W), lambda i: (0, 0))],
        out_specs=pl.BlockSpec((4, tm, HEAD_W), lambda i: (0, i, 0)),
        out_shape=jax.ShapeDtypeStruct((4, t, HEAD_W), F32),
        compiler_params=_cparams(("parallel",)),
        name="rope_tables",
    )(pos_col, ftab)


PROJ_ROWS = 256
B_KINDS = (K_ROPE_B, K_ROPE_B, K_PLAIN)
PROJ_TILES = (tuple((k, 0) for k in (K_ROPE_A, K_ROPE_A, K_PLAIN) + B_KINDS + (K_GATE,) * 4)
              + tuple((k, 1) for k in B_KINDS) + tuple((k, 2) for k in B_KINDS))


def _proj_kernel(layer_ref, x_ref, g_ref, w_ref, tab1_ref, tab4_ref, tab16_ref,
                 o1_ref, o4_ref, o16_ref, h_scr, acc_scr, *, tm):
    rc = PROJ_ROWS
    h_scr[...] = _rms(x_ref[...], g_ref[...]).astype(BF16)
    tabs = (tab1_ref, tab4_ref, tab16_ref)
    outs = (o1_ref, o4_ref, o16_ref)

    def finish(kind, a, cos_sin):
        if kind in (K_ROPE_A, K_ROPE_B):
            cos, sin = cos_sin()
            out = a * cos + pltpu.roll(a, HEAD_W // 2, 1) * sin
        elif kind == K_GATE:
            out = jax.nn.sigmoid(a)
        else:
            out = a
        return out.astype(BF16)

    first = {}
    for ct, (kind, grp) in enumerate(PROJ_TILES):
        first.setdefault(grp, ct)
        head0 = (ct - first[grp]) * N_HEADS
        dil = B_PAIRS[grp][1]
        n = rc // dil
        tab_ref, o_ref = tabs[grp], outs[grp]
        t0 = 2 if (kind == K_ROPE_B and grp == 0) else 0
        for c in range(tm // rc):
            rsl = slice(c * rc, (c + 1) * rc)
            acc = jnp.dot(h_scr[rsl, :], w_ref[:, ct * COL_TILE:(ct + 1) * COL_TILE],
                          preferred_element_type=F32)
            slot = (ct + c) % 2
            for hh in range(N_HEADS):
                a = acc[:, hh * HEAD_W:(hh + 1) * HEAD_W]
                if dil == 1:
                    o_ref[head0 + hh, 0, rsl, :] = finish(
                        kind, a, lambda: (tab_ref[t0, rsl, :], tab_ref[t0 + 1, rsl, :]))
                else:
                    acc_scr[slot, hh] = a
            if dil > 1:
                usl = slice(c * n, (c + 1) * n)
                for hh in range(N_HEADS):
                    for r in range(dil):
                        a = acc_scr[slot, hh, pl.ds(r, n, stride=dil), :]
                        o_ref[head0 + hh, r, usl, :] = finish(
                            kind, a, lambda: (tab_ref[0, r, usl, :], tab_ref[1, r, usl, :]))


def _layer_block(shape, ngrid, col=0):
    return pl.BlockSpec((None,) + shape,
                        lambda *a: (a[ngrid][0],) + (0,) * (len(shape) - 1) + (col,),
                        pipeline_mode=pl.Buffered(1))


def _project(layer, xf, g, w, tabs, tabs4, tabs16, *, batch, seq, tm=512):
    nrow = seq // tm
    d4, d16 = B_PAIRS[1][1], B_PAIRS[2][1]
    n1 = sum(1 for _, grp in PROJ_TILES if grp == 0) * N_HEADS
    nd = len(B_KINDS) * N_HEADS
    res = lambda lead, dil: pl.BlockSpec((lead, None, dil, tm // dil, HEAD_W),
                                         lambda b, i, l: (0, b, 0, i, 0))
    shp = lambda lead, dil: jax.ShapeDtypeStruct((lead, batch, dil, seq // dil, HEAD_W), BF16)
    grid_spec = pltpu.PrefetchScalarGridSpec(
        num_scalar_prefetch=1,
        grid=(batch, nrow),
        in_specs=[
            pl.BlockSpec((tm, D_MODEL), lambda b, i, l: (b * nrow + i, 0)),
            _layer_block((1, D_MODEL), 2),
            _layer_block((D_MODEL, len(PROJ_TILES) * COL_TILE), 2),
            pl.BlockSpec((4, tm, HEAD_W), lambda b, i, l: (0, b * nrow + i, 0)),
            res(2, d4), res(2, d16),
        ],
        out_specs=[res(n1, 1), res(nd, d4), res(nd, d16)],
        scratch_shapes=[pltpu.VMEM((tm, D_MODEL), BF16),
                        pltpu.VMEM((2, N_HEADS, PROJ_ROWS, HEAD_W), F32)],
    )
    return pl.pallas_call(
        functools.partial(_proj_kernel, tm=tm),
        grid_spec=grid_spec,
        out_shape=[shp(n1, 1), shp(nd, d4), shp(nd, d16)],
        compiler_params=_cparams(("parallel", "parallel")),
        name="in_proj",
    )(layer, xf, g, w, tabs, tabs4, tabs16)


def _diff_attn_body(lam_ref, g_ref, q_ref, k_ref, v_ref, o_ref, m_scr, l_scr, acc_scr, *, tk, qs):
    tq = q_ref.shape[0]
    q = q_ref[...]
    comp = (lax.broadcasted_iota(jnp.int32, q.shape, 1) // (A_QK_DIM // 2)) % 2
    zero = jnp.zeros_like(q)
    qc = (jnp.where(comp == 0, q, zero), jnp.where(comp == 1, q, zero))
    ones = jnp.ones((tk, HEAD_W), BF16)

    def block(j, d):
        base = j * tk
        for r in range(tq // qs):
            rows = slice(r * qs, (r + 1) * qs)
            off = 0 if d is None else r * qs - d * tk
            if d is not None and off + qs <= 0:
                continue
            masked = d is not None and off < tk - 1
            w = min(tk, off + qs) if masked else tk
            k = k_ref[pl.ds(base, w), :]
            vext = jnp.concatenate([v_ref[pl.ds(base, w), :], ones[:w]], axis=1)
            for c in range(2):
                s = lax.dot_general(qc[c][rows], k, (((1,), (1,)), ((), ())),
                                    preferred_element_type=F32)
                if masked:
                    row = lax.broadcasted_iota(jnp.int32, s.shape, 0) + off
                    col = lax.broadcasted_iota(jnp.int32, s.shape, 1)
                    s = jnp.where(col <= row, s, -jnp.inf)
                if d == 0:
                    m_new = jnp.broadcast_to(jnp.max(s, axis=-1, keepdims=True), (qs, HEAD_W))
                    p = jnp.exp(s - jnp.tile(m_new, (1, w // HEAD_W)))
                    pv = jnp.dot(p.astype(BF16), vext, preferred_element_type=F32)
                    l_scr[c, rows, :] = pv[:, HEAD_W:]
                    acc_scr[c, rows, :] = pv[:, :HEAD_W]
                    m_scr[c, rows, :] = m_new
                    continue
                m_prev = m_scr[c, rows, :]
                m_new = jnp.maximum(m_prev, jnp.max(s, axis=-1, keepdims=True))
                p = jnp.exp(s - jnp.tile(m_new, (1, w // HEAD_W)))
                alpha = jnp.exp(m_prev - m_new)
                pv = jnp.dot(p.astype(BF16), vext, preferred_element_type=F32)
                l_scr[c, rows, :] = alpha * l_scr[c, rows, :] + pv[:, HEAD_W:]
                acc_scr[c, rows, :] = alpha * acc_scr[c, rows, :] + pv[:, :HEAD_W]
                m_scr[c, rows, :] = m_new

    for d in range(tq // tk):
        block(d, d)

    lv = lam_ref[...]
    lam_init = lv[4:5, 0:1]
    lam = (jnp.exp(jnp.sum(lv[0:1] * lv[1:2], axis=-1, keepdims=True))
           - jnp.exp(jnp.sum(lv[2:3] * lv[3:4], axis=-1, keepdims=True)) + lam_init)
    for r in range(tq // qs):
        rows = slice(r * qs, (r + 1) * qs)
        o = acc_scr[0, rows, :] / l_scr[0, rows, :] - lam * (acc_scr[1, rows, :] / l_scr[1, rows, :])
        o_ref[rows, :] = (_rms(o, g_ref[...]) * (1.0 - lam_init)).astype(BF16)


DIL_SPAN = B_BLOCK * B_PAIRS[-1][1]


def _band_bias(has_prev):
    shape = (B_BLOCK, 2 * B_BLOCK)
    qi = lax.broadcasted_iota(jnp.int32, shape, 0)
    kj = lax.broadcasted_iota(jnp.int32, shape, 1)
    valid = (kj >= qi) & (kj <= qi + B_BLOCK)
    if not has_prev:
        valid = valid & (kj >= B_BLOCK)
    return jnp.where(valid, 0.0, -jnp.inf).astype(F32)


def _band_unit(q, kk, vv, bias):
    s = lax.dot_general(q, kk, (((1,), (1,)), ((), ())), preferred_element_type=F32) + bias
    m = jnp.max(s, axis=-1, keepdims=True)
    scale = HEAD_W ** -0.5
    p = jnp.exp2((s - m) * (scale * math.log2(math.e)))
    ones = jnp.ones((2 * B_BLOCK, HEAD_W), BF16)
    pv = jnp.dot(p.astype(BF16), jnp.concatenate([vv, ones], axis=1), preferred_element_type=F32)
    den = pv[:, HEAD_W:]
    return pv[:, :HEAD_W] / den, m * scale + jnp.log(den)


def _dilated_body(groups, o_ref, o_scr, lse_scr):
    seq = o_ref.shape[0]
    bias = {False: _band_bias(False), True: _band_bias(True)}

    for span in range(seq // DIL_SPAN):
        for g, (q, k, v) in enumerate(groups):
            dil = B_PAIRS[g][1]
            nblk = DIL_SPAN // (dil * B_BLOCK)
            for r in range(dil):
                at = (lambda ref, sl: ref[sl, :]) if g == 0 else (lambda ref, sl, r=r: ref[r, sl, :])
                for ub in range(nblk):
                    n = span * nblk + ub
                    cur = slice(n * B_BLOCK, (n + 1) * B_BLOCK)
                    if n == 0:
                        kk = jnp.concatenate([at(k, cur)] * 2, axis=0)
                        vv = jnp.concatenate([at(v, cur)] * 2, axis=0)
                    else:
                        both = slice((n - 1) * B_BLOCK, (n + 1) * B_BLOCK)
                        kk, vv = at(k, both), at(v, both)
                    o, lse = _band_unit(at(q, cur), kk, vv, bias[n > 0])
                    rows = pl.ds(ub * B_BLOCK * dil + r, B_BLOCK, stride=dil) if dil > 1 else \
                        slice(ub * B_BLOCK, (ub + 1) * B_BLOCK)
                    o_scr[g, rows, :] = o
                    lse_scr[g, rows, :] = lse

        chunk = 256
        for c in range(DIL_SPAN // chunk):
            rows = slice(c * chunk, (c + 1) * chunk)
            ls = [lse_scr[g, rows, :] for g in range(3)]
            mx = jnp.maximum(jnp.maximum(ls[0], ls[1]), ls[2])
            es = [jnp.exp(l - mx) for l in ls]
            num = es[0] * o_scr[0, rows, :] + es[1] * o_scr[1, rows, :] + es[2] * o_scr[2, rows, :]
            out_rows = slice(span * DIL_SPAN + c * chunk, span * DIL_SPAN + (c + 1) * chunk)
            o_ref[out_rows, :] = (num / (es[0] + es[1] + es[2])).astype(BF16)


def _diff_attn_kernel(lam_ref, g_ref, qa, ka, va, oa_ref, m_scr, l_scr, acc_scr):
    _diff_attn_body(lam_ref, g_ref, qa, ka, va, oa_ref, m_scr, l_scr, acc_scr, tk=512, qs=512)


def _dilated_kernel(q1, k1, v1, q2, k2, v2, q3, k3, v3, ob_ref, o_scr, lse_scr):
    _dilated_body(((q1, k1, v1), (q2, k2, v2), (q3, k3, v3)), ob_ref, o_scr, lse_scr)


def _token_mixers(pa, pb, pc, lam_p, head_g, *, batch, seq, lam_init):
    lam_p = jnp.concatenate([lam_p, jnp.full((1, A_QK_DIM), lam_init, F32)], axis=0)

    def qkv_specs(bases, dil):
        shape = (None, None, None, seq, HEAD_W) if dil == 1 else (None, None, dil, seq // dil, HEAD_W)
        return [pl.BlockSpec(shape, lambda b, h, base=base: (base + h, b, 0, 0, 0)) for base in bases]

    out_spec = pl.BlockSpec((None, seq, HEAD_W), lambda b, h: (h, b, 0))
    out_shape = jax.ShapeDtypeStruct((N_HEADS, batch * seq, HEAD_W), BF16)
    a_bases = (0, N_HEADS, 2 * N_HEADS)
    g0_bases = (3 * N_HEADS, 4 * N_HEADS, 5 * N_HEADS)
    oa = pl.pallas_call(
        _diff_attn_kernel,
        grid=(batch, N_HEADS),
        in_specs=[pl.BlockSpec((5, A_QK_DIM), lambda b, h: (0, 0)),
                  pl.BlockSpec((1, HEAD_W), lambda b, h: (0, 0))] + qkv_specs(a_bases, 1),
        out_specs=out_spec,
        out_shape=out_shape,
        scratch_shapes=[pltpu.VMEM((2, seq, HEAD_W), F32)] * 3,
        compiler_params=_cparams(("parallel", "parallel")),
        name="diff_attention",
    )(lam_p, head_g, pa, pa, pa)
    ob = pl.pallas_call(
        _dilated_kernel,
        grid=(batch, N_HEADS),
        in_specs=(qkv_specs(g0_bases, 1) + qkv_specs(a_bases, B_PAIRS[1][1])
                  + qkv_specs(a_bases, B_PAIRS[2][1])),
        out_specs=out_spec,
        out_shape=out_shape,
        scratch_shapes=[pltpu.VMEM((3, DIL_SPAN, HEAD_W), F32)] * 2,
        compiler_params=_cparams(("parallel", "parallel")),
        name="dilated_attention",
    )(pa, pa, pa, pb, pb, pb, pc, pc, pc)
    return oa, ob


MERGE_ROWS = 512


def _merge_kernel(layer_ref, x_ref, oa_ref, ob_ref, ga_ref, gb_ref, wa_ref, wb_ref, wm_ref, g_ref,
                  o_ref):
    ntile = D_MODEL // HEAD_W
    for c in range(x_ref.shape[0] // MERGE_ROWS):
        rs = slice(c * MERGE_ROWS, (c + 1) * MERGE_ROWS)
        heads = lambda ref, n: jnp.concatenate([ref[k, rs, :] for k in range(n)], axis=-1)
        ya = jnp.dot(heads(oa_ref, N_HEADS), wa_ref[...], preferred_element_type=F32)
        yb = jnp.dot(heads(ob_ref, N_HEADS), wb_ref[...], preferred_element_type=F32)
        z = heads(ga_ref, ntile).astype(F32) * ya + heads(gb_ref, ntile).astype(F32) * yb
        mix = jnp.dot(z.astype(BF16), wm_ref[...], preferred_element_type=F32)
        o_ref[rs, :] = x_ref[rs, :] + _rms(mix, g_ref[...])


def _merge(layer, xf, oa, ob, pa, wa, wb, wm, g, *, batch, seq, tm=2 * MERGE_ROWS):
    t = batch * seq
    nrow = seq // tm
    ntile = D_MODEL // HEAD_W
    gate_base = 6 * N_HEADS // ntile
    gate = lambda off: pl.BlockSpec((ntile, None, None, tm, HEAD_W),
                                    lambda i, l: (gate_base + off, i // nrow, 0, i % nrow, 0))
    grid_spec = pltpu.PrefetchScalarGridSpec(
        num_scalar_prefetch=1,
        grid=(t // tm,),
        in_specs=[
            pl.BlockSpec((tm, D_MODEL), lambda i, l: (i, 0)),
            pl.BlockSpec((N_HEADS, tm, HEAD_W), lambda i, l: (0, i, 0)),
            pl.BlockSpec((N_HEADS, tm, HEAD_W), lambda i, l: (0, i, 0)),
            gate(0), gate(1),
            _layer_block((N_HEADS * HEAD_W, D_MODEL), 1), _layer_block((N_HEADS * HEAD_W, D_MODEL), 1),
            _layer_block((D_MODEL, D_MODEL), 1), _layer_block((1, D_MODEL), 1),
        ],
        out_specs=pl.BlockSpec((tm, D_MODEL), lambda i, l: (i, 0)),
    )
    return pl.pallas_call(
        _merge_kernel,
        grid_spec=grid_spec,
        out_shape=jax.ShapeDtypeStruct((t, D_MODEL), F32),
        compiler_params=_cparams(("parallel",)),
        name="merge",
    )(layer, xf, oa, ob, pa, pa, wa, wb, wm, g)


FFN_HALO = 16


def _ffn_kernel(layer_ref, x_ref, xh_ref, gpre_ref, wg_ref, wv_ref, cw_ref, cb_ref, wd_ref, gpost_ref,
                o_ref, h_scr, *, tiles_per_seq):
    i = pl.program_id(0)
    hh = _rms(xh_ref[...], gpre_ref[...])
    hh = jnp.where(i % tiles_per_seq == 0, jnp.zeros_like(hh), hh)
    h_scr[0:FFN_HALO, :] = hh.astype(BF16)
    h_scr[FFN_HALO:, :] = _rms(x_ref[...], gpre_ref[...]).astype(BF16)

    def conv(u, cw, cb):
        u1 = pltpu.roll(u, 1, 0)[FFN_HALO:, :]
        u2 = pltpu.roll(u, 2, 0)[FFN_HALO:, :]
        return cw[2:3, :] * u[FFN_HALO:, :] + cw[1:2, :] * u1 + cw[0:1, :] * u2 + cb

    h = h_scr[...]
    gate = conv(jnp.dot(h, wg_ref[...], preferred_element_type=F32),
                cw_ref[:, :D_FF], cb_ref[:, :D_FF])
    half_val = conv(jnp.dot(h, wv_ref[...], preferred_element_type=F32),
                    0.5 * cw_ref[:, D_FF:], 0.5 * cb_ref[:, D_FF:])
    c = math.sqrt(2.0 / math.pi)
    inner = gate * (c + (c * 0.044715) * (gate * gate))
    act = ((gate * half_val) * (1.0 + jnp.tanh(inner))).astype(BF16)
    y = jnp.dot(act, wd_ref[...], preferred_element_type=F32)
    o_ref[...] = x_ref[...] + _rms(y, gpost_ref[...])


def _ffn(layer, xf, gpre, w_up, conv_w, conv_b, w_down, gpost, *, seq, tm=512):
    t = xf.shape[0]
    tiles_per_seq = seq // tm
    hb = tm // FFN_HALO
    kern = functools.partial(_ffn_kernel, tiles_per_seq=tiles_per_seq)
    const = lambda shape, col=0: _layer_block(shape, 1, col)
    grid_spec = pltpu.PrefetchScalarGridSpec(
        num_scalar_prefetch=1,
        grid=(t // tm,),
        in_specs=[
            pl.BlockSpec((tm, D_MODEL), lambda i, l: (i, 0)),
            pl.BlockSpec((FFN_HALO, D_MODEL), lambda i, l: (jnp.maximum(i * hb - 1, 0), 0)),
            const((1, D_MODEL)),
            const((D_MODEL, D_FF), 0), const((D_MODEL, D_FF), 1),
            const((3, 2 * D_FF)), const((1, 2 * D_FF)),
            const((D_FF, D_MODEL)), const((1, D_MODEL)),
        ],
        out_specs=pl.BlockSpec((tm, D_MODEL), lambda i, l: (i, 0)),
        scratch_shapes=[pltpu.VMEM((tm + FFN_HALO, D_MODEL), BF16)],
    )
    return pl.pallas_call(
        kern,
        grid_spec=grid_spec,
        out_shape=jax.ShapeDtypeStruct((t, D_MODEL), F32),
        compiler_params=_cparams(("parallel",)),
        name="conv_glu",
    )(layer, xf, xf, gpre, w_up, w_up, conv_w, conv_b, w_down, gpost)


def kernel(x, positions, pre_mix_g, w_in, diff_lambda, diff_head_g, w_a_out, w_b_out, w_mix_out,
           post_mix_g, pre_ffn_g, w_up, conv_w, conv_b, w_down, post_ffn_g):
    batch, seq, d = x.shape
    assert d == D_MODEL and seq % DIL_SPAN == 0
    depth = w_in.shape[0]
    t = batch * seq
    xf = x.reshape(t, d)
    tabs = _rope_tables(positions.reshape(t, 1))
    d4, d16 = B_PAIRS[1][1], B_PAIRS[2][1]
    res_major = lambda dil: tabs[2:].reshape(2, batch, seq // dil, dil, HEAD_W).swapaxes(2, 3)
    tabs_d4, tabs_d16 = res_major(d4), res_major(d16)

    a_w = N_HEADS * HEAD_W
    b0 = 3 * a_w
    gsl = lambda part, g: slice(b0 + (3 * part + g) * a_w, b0 + (3 * part + g + 1) * a_w)
    row = lambda v: v.reshape(1, -1)
    half = A_QK_DIM // 2

    def regroup_a(w):
        return w.reshape(depth, d, N_HEADS, 2, 2, half).swapaxes(3, 4).reshape(depth, d, a_w)

    w_all = jnp.concatenate(
        [regroup_a(w_in[:, :, :a_w]) * (A_QK_DIM ** -0.5), regroup_a(w_in[:, :, a_w:2 * a_w]),
         w_in[:, :, 2 * a_w:b0], w_in[:, :, gsl(0, 0)], w_in[:, :, gsl(1, 0)], w_in[:, :, gsl(2, 0)],
         w_in[:, :, b0 + 9 * a_w:]]
        + [w_in[:, :, gsl(p, 1)] for p in range(3)] + [w_in[:, :, gsl(p, 2)] for p in range(3)],
        axis=2).astype(BF16)
    rows = lambda v: v.reshape(depth, 1, -1)
    wa, wb, wm = w_a_out.astype(BF16), w_b_out.astype(BF16), w_mix_out.astype(BF16)
    wu, wd = w_up.astype(BF16), w_down.astype(BF16)
    g_pre, g_post, g_pre_ffn, g_post_ffn = (rows(pre_mix_g), rows(post_mix_g), rows(pre_ffn_g),
                                            rows(post_ffn_g))
    cb = rows(conv_b)

    for l in range(depth):
        layer = jnp.full((1,), l, jnp.int32)
        pa, pb, pc = _project(layer, xf, g_pre, w_all, tabs, tabs_d4, tabs_d16, batch=batch, seq=seq)
        lam_init = 0.8 - 0.6 * math.exp(-0.3 * l)
        oa, ob = _token_mixers(pa, pb, pc, diff_lambda[l], row(diff_head_g[l]), batch=batch, seq=seq,
                               lam_init=lam_init)
        xf = _merge(layer, xf, oa, ob, pa, wa, wb, wm, g_post, batch=batch, seq=seq)
        xf = _ffn(layer, xf, g_pre_ffn, wu, conv_w, cb, wd, g_post_ffn, seq=seq)
    return xf.reshape(batch, seq, d)
```
